```python
import jax, jax.numpy as jnp
from jax import lax
import numpy as np

D_MODEL = 1024
BATCH = 4
SEQ = 8192
DEPTH = 4

N_MIXERS = 3
GLA_HEADS = 4
GLA_DK = D_MODEL // 2
GLA_DV = D_MODEL
GLA_HEAD_K = GLA_DK // GLA_HEADS
GLA_HEAD_V = GLA_DV // GLA_HEADS
GLA_GATE_RANK = 16
GLA_GATE_NORMALIZER = 16.0
GLA_CHUNK = 64
GLA_IN_COLS = 2 * GLA_DK + 2 * GLA_DV + GLA_GATE_RANK
SBA_HEADS = 16
SBA_HEAD_DIM = D_MODEL // SBA_HEADS
SBA_BLOCK = 128
POOL_WINDOWS = (2, 4, 8, 16)
POOL_GROUPS = len(POOL_WINDOWS)
POOL_GROUP_DIM = D_MODEL // POOL_GROUPS
D_FF = 2816
N_EXPERTS = 8
TOP_K = 2
RMS_EPS = 1e-6
N_GLA = (DEPTH + 2) // 3
N_SBA = (DEPTH + 1) // 3
N_POOL = DEPTH // 3
N_DENSE = (DEPTH + 1) // 2
N_MOE = DEPTH // 2

kernel_name = "hybrid_gla_stickbreak_pool_moe"

F32 = jnp.float32


def rms_norm(x, gain):
    xf = x.astype(F32)
    y = xf * lax.rsqrt(jnp.mean(xf * xf, axis=-1, keepdims=True) + RMS_EPS) * gain.astype(F32)
    return y.astype(x.dtype)


def _heads(t, n_heads):
    B, S, _ = t.shape
    return t.reshape(B, S, n_heads, -1).transpose(0, 2, 1, 3)


def gla_chunked(q, k, v, log_alpha):
    B, H, S, dk = q.shape
    dv = v.shape[-1]
    C = GLA_CHUNK
    N = S // C

    def chunks(t):
        return t.reshape(B, H, N, C, t.shape[-1]).transpose(2, 0, 1, 3, 4).astype(F32)

    mask = jnp.tril(jnp.ones((C, C), dtype=bool))[:, :, None]

    def step(state, inp):
        qc, kc, vc, gc = inp
        b = jnp.cumsum(gc, axis=2)
        diff = b[:, :, :, None, :] - b[:, :, None, :, :]
        decay = jnp.where(mask, jnp.exp(jnp.where(mask, diff, 0.0)), 0.0)
        scores = jnp.einsum('bhik,bhjk,bhijk->bhij', qc, kc, decay)
        o = jnp.einsum('bhij,bhjv->bhiv', scores, vc) + \
            jnp.einsum('bhik,bhkv->bhiv', qc * jnp.exp(b), state)
        b_last = b[:, :, -1:, :]
        state = jnp.exp(b_last[:, :, 0, :])[..., None] * state + \
            jnp.einsum('bhjk,bhjv->bhkv', kc * jnp.exp(b_last - b), vc)
        return state, o

    s0 = jnp.zeros((B, H, dk, dv), F32)
    _, o = lax.scan(step, s0, (chunks(q), chunks(k), chunks(v), chunks(log_alpha)))
    return o.transpose(1, 2, 0, 3, 4).reshape(B, H, S, dv)


def gla_mixer(h, w_in, w_gate_up, b_gate, norm_gain, w_out):
    B, S, _ = h.shape
    q, k, v, g, a = jnp.split(h @ w_in, [GLA_DK, 2 * GLA_DK, 2 * GLA_DK + GLA_DV,
                                          2 * GLA_DK + 2 * GLA_DV], axis=-1)
    log_alpha = jax.nn.log_sigmoid((a @ w_gate_up + b_gate).astype(F32)) / GLA_GATE_NORMALIZER
    o = gla_chunked(_heads(q, GLA_HEADS) * GLA_HEAD_K ** -0.5, _heads(k, GLA_HEADS),
                    _heads(v, GLA_HEADS), _heads(log_alpha, GLA_HEADS))
    o = o.transpose(0, 2, 1, 3)
    o = rms_norm(o, norm_gain) * jax.nn.silu(g.reshape(B, S, GLA_HEADS, GLA_HEAD_V)).astype(F32)
    return o.reshape(B, S, GLA_DV).astype(h.dtype) @ w_out


def stick_breaking(q, k, v):
    B, H, S, dh = q.shape
    nb = S // SBA_BLOCK
    qb = q.reshape(B, H, nb, SBA_BLOCK, dh).transpose(2, 0, 1, 3, 4)
    kf = k.astype(F32)
    vf = v.astype(F32)
    key_pos = jnp.arange(S)
    scale = dh ** -0.5

    def block(args):
        q_blk, blk = args
        q_pos = blk * SBA_BLOCK + jnp.arange(SBA_BLOCK)
        causal = key_pos[None, :] < q_pos[:, None]
        z = jnp.einsum('bhqd,bhsd->bhqs', q_blk.astype(F32), kf) * scale
        log_fail = jnp.where(causal, jax.nn.log_sigmoid(-z), 0.0)
        later = lax.cumsum(log_fail, axis=3, reverse=True) - log_fail
        w = jnp.where(causal, jnp.exp(jax.nn.log_sigmoid(z) + later), 0.0)
        return jnp.einsum('bhqs,bhsd->bhqd', w, vf)

    out = lax.map(block, (qb, jnp.arange(nb)))
    return out.transpose(1, 2, 0, 3, 4).reshape(B, H, S, dh)


def sba_mixer(h, w_in, w_out):
    B, S, _ = h.shape
    q, k, v = jnp.split(h @ w_in, 3, axis=-1)
    o = stick_breaking(_heads(q, SBA_HEADS), _heads(k, SBA_HEADS), _heads(v, SBA_HEADS))
    o = o.transpose(0, 2, 1, 3).reshape(B, S, D_MODEL).astype(h.dtype)
    return o @ w_out


def causal_window_mean(u, w):
    S = u.shape[1]
    cs = jnp.cumsum(u, axis=1)
    shifted = jnp.pad(cs, ((0, 0), (w, 0), (0, 0)))[:, :S]
    count = jnp.minimum(jnp.arange(1, S + 1), w).astype(F32)
    return (cs - shifted) / count[None, :, None]


def pool_mixer(h, w_in, w_group, scale):
    B, S, _ = h.shape
    u = (h @ w_in).astype(F32).reshape(B, S, POOL_GROUPS, POOL_GROUP_DIM)
    pooled = jnp.stack([causal_window_mean(u[:, :, gi], w) - u[:, :, gi]
                        for gi, w in enumerate(POOL_WINDOWS)], axis=2)
    mixed = jnp.einsum('bsgc,gcd->bsgd', pooled, w_group.astype(F32)).reshape(B, S, D_MODEL)
    return (mixed * scale.astype(F32)).astype(h.dtype)


def swiglu(t, w_gate, w_up, w_down):
    return (jax.nn.silu(t @ w_gate) * (t @ w_up)) @ w_down


def moe(h, router, w_gate, w_up, w_down):
    B, S, D = h.shape
    t = h.reshape(-1, D)
    logits = (t @ router).astype(F32)
    top_v, top_i = lax.top_k(logits, TOP_K)
    probs = jax.nn.softmax(top_v, axis=-1)
    combine = jnp.sum(jax.nn.one_hot(top_i, N_EXPERTS, dtype=F32) * probs[..., None], axis=1)
    y = jnp.zeros_like(t)
    for e in range(N_EXPERTS):
        y = y + combine[:, e:e + 1].astype(t.dtype) * swiglu(t, w_gate[e], w_up[e], w_down[e])
    return y.reshape(B, S, D)


def _dense(key, shape, fan_in):
    return jax.random.normal(key, shape, F32) * fan_in ** -0.5


def setup_inputs(seed: int = 0) -> dict:
    key = jax.random.key(seed)
    ks = jax.random.split(key, 21)
    D = D_MODEL
    return {
        "x": jax.random.normal(ks[0], (BATCH, SEQ, D), F32),
        "norm_mix": 1.0 + 0.02 * jax.random.normal(ks[1], (DEPTH, D), F32),
        "norm_ffn": 1.0 + 0.02 * jax.random.normal(ks[2], (DEPTH, D), F32),
        "norm_final": 1.0 + 0.02 * jax.random.normal(ks[3], (D,), F32),
        "gla_w_in": _dense(ks[4], (N_GLA, D, GLA_IN_COLS), D),
        "gla_w_gate_up": _dense(ks[5], (N_GLA, GLA_GATE_RANK, GLA_DK), GLA_GATE_RANK),
        "gla_b_gate": 0.1 * jax.random.normal(ks[6], (N_GLA, GLA_DK), F32),
        "gla_norm": 1.0 + 0.02 * jax.random.normal(ks[7], (N_GLA, GLA_HEAD_V), F32),
        "gla_w_out": _dense(ks[8], (N_GLA, GLA_DV, D), GLA_DV),
        "sba_w_in": _dense(ks[9], (N_SBA, D, 3 * D), D),
        "sba_w_out": _dense(ks[10], (N_SBA, D, D), D),
        "pool_w_in": _dense(ks[11], (N_POOL, D, D), D),
        "pool_w_group": _dense(ks[12], (N_POOL, POOL_GROUPS, POOL_GROUP_DIM, POOL_GROUP_DIM), POOL_GROUP_DIM),
        "pool_scale": 1.0 + 0.02 * jax.random.normal(ks[13], (N_POOL, D), F32),
        "ffn_w_gate": _dense(ks[14], (N_DENSE, D, D_FF), D),
        "ffn_w_up": _dense(ks[15], (N_DENSE, D, D_FF), D),
        "ffn_w_down": _dense(ks[16], (N_DENSE, D_FF, D), D_FF),
        "moe_router": _dense(ks[17], (N_MOE, D, N_EXPERTS), D),
        "moe_w_gate": _dense(ks[18], (N_MOE, N_EXPERTS, D, D_FF), D),
        "moe_w_up": _dense(ks[19], (N_MOE, N_EXPERTS, D, D_FF), D),
        "moe_w_down": _dense(ks[20], (N_MOE, N_EXPERTS, D_FF, D), D_FF),
    }


def reference(x, norm_mix, norm_ffn, norm_final, gla_w_in, gla_w_gate_up, gla_b_gate,
              gla_norm, gla_w_out, sba_w_in, sba_w_out, pool_w_in, pool_w_group, pool_scale,
              ffn_w_gate, ffn_w_up, ffn_w_down, moe_router, moe_w_gate, moe_w_up, moe_w_down):
    for i in range(DEPTH):
        h = rms_norm(x, norm_mix[i])
        kind, j = i % N_MIXERS, i // N_MIXERS
        if kind == 0:
            mix = gla_mixer(h, gla_w_in[j], gla_w_gate_up[j], gla_b_gate[j], gla_norm[j], gla_w_out[j])
        elif kind == 1:
            mix = sba_mixer(h, sba_w_in[j], sba_w_out[j])
        else:
            mix = pool_mixer(h, pool_w_in[j], pool_w_group[j], pool_scale[j])
        x = x + mix.astype(x.dtype)
        h = rms_norm(x, norm_ffn[i])
        f = i // 2
        if i % 2 == 0:
            y = swiglu(h, ffn_w_gate[f], ffn_w_up[f], ffn_w_down[f])
        else:
            y = moe(h, moe_router[f], moe_w_gate[f], moe_w_up[f], moe_w_down[f])
        x = x + y.astype(x.dtype)
    return rms_norm(x, norm_final)
```

```python
import functools

import numpy as np
import jax
import jax.numpy as jnp
from jax import lax
from jax.experimental import pallas as pl
from jax.experimental.pallas import tpu as pltpu

F32 = jnp.float32
BF16 = jnp.bfloat16
I32 = jnp.int32
HIGHEST = lax.Precision.HIGHEST

RMS_EPS = 1e-6
GLA_HEADS = 4
GLA_GATE_RANK = 16
GLA_GATE_NORMALIZER = 16.0
SBA_HEADS = 16
POOL_WINDOWS = (2, 4, 8, 16)
N_EXPERTS = 8

LANE = 128
VMEM_LIMIT = 56 * 1024 * 1024
ROW_TILE = 512
FF_CHUNK = 256
GLA_CHUNK = 64
GLA_SUB = 16
GLA_ROWS = 512
SBA_BQ = 256
SBA_BK = 128
SBA_DEAD = -104.0
MOE_TILE = 512
GATHER_ROWS = 512


def _cparams(n_axes):
    return pltpu.CompilerParams(dimension_semantics=("arbitrary",) * n_axes,
                                vmem_limit_bytes=VMEM_LIMIT)


def _rms(xf, gain):
    return xf * lax.rsqrt(jnp.mean(xf * xf, axis=-1, keepdims=True) + RMS_EPS) * gain


def _sigmoid(x):
    return 1.0 / (1.0 + jnp.exp(-x))


def _log_sigmoid(z):
    return jnp.minimum(z, 0.0) - jnp.log(1.0 + jnp.exp(-jnp.abs(z)))


def _const_spec(shape):
    return pl.BlockSpec(shape, lambda *_: (0,) * len(shape))


def _norm_proj_kernel(x_ref, g_ref, w_ref, o_ref, *, nchunk):
    h = _rms(x_ref[...], g_ref[...]).astype(BF16)
    n = o_ref.shape[1]
    for c0 in range(0, n, nchunk):
        o_ref[:, c0:c0 + nchunk] = jnp.dot(
            h, w_ref[:, c0:c0 + nchunk], preferred_element_type=F32).astype(o_ref.dtype)


def norm_proj(x, gain, w, tm=ROW_TILE):
    t, d = x.shape
    n = w.shape[1]
    return pl.pallas_call(
        functools.partial(_norm_proj_kernel, nchunk=512),
        grid=(t // tm,),
        in_specs=[pl.BlockSpec((tm, d), lambda i: (i, 0)),
                  _const_spec((1, d)), _const_spec((d, n))],
        out_specs=pl.BlockSpec((tm, n), lambda i: (i, 0)),
        out_shape=jax.ShapeDtypeStruct((t, n), BF16),
        compiler_params=_cparams(1), name="norm_proj",
    )(x, gain.reshape(1, d), w)


def _gla_proj_kernel(x_ref, g_ref, w_ref, wa_ref, wgu_ref, bg_ref, o_ref, la_ref, *, nchunk):
    hf = _rms(x_ref[...], g_ref[...])
    h = hf.astype(BF16)
    n = o_ref.shape[1]
    for c0 in range(0, n, nchunk):
        o_ref[:, c0:c0 + nchunk] = jnp.dot(
            h, w_ref[:, c0:c0 + nchunk], preferred_element_type=F32).astype(o_ref.dtype)
    a = jnp.dot(hf, wa_ref[...], precision=HIGHEST, preferred_element_type=F32)
    z = jnp.dot(a, wgu_ref[...], precision=HIGHEST, preferred_element_type=F32) + bg_ref[...]
    la_ref[...] = _log_sigmoid(z) / GLA_GATE_NORMALIZER


def gla_proj(x, gain, w_main, w_a, w_gu, b_gate, tm=ROW_TILE):
    t, d = x.shape
    n = w_main.shape[1]
    dk = w_gu.shape[1]
    return pl.pallas_call(
        functools.partial(_gla_proj_kernel, nchunk=512),
        grid=(t // tm,),
        in_specs=[pl.BlockSpec((tm, d), lambda i: (i, 0)),
                  _const_spec((1, d)), _const_spec((d, n)),
                  _const_spec((d, LANE)), _const_spec((LANE, dk)), _const_spec((1, dk))],
        out_specs=[pl.BlockSpec((tm, n), lambda i: (i, 0)),
                   pl.BlockSpec((tm, dk), lambda i: (i, 0))],
        out_shape=[jax.ShapeDtypeStruct((t, n), BF16), jax.ShapeDtypeStruct((t, dk), F32)],
        compiler_params=_cparams(1), name="gla_proj",
    )(x, gain.reshape(1, d), w_main, w_a, w_gu, b_gate.reshape(1, dk))


def _proj_residual_kernel(x_ref, a_ref, w_ref, o_ref):
    o_ref[...] = x_ref[...] + jnp.dot(a_ref[...], w_ref[...], preferred_element_type=F32)


def proj_residual(x, a, w, tm=ROW_TILE):
    t, d = x.shape
    k = a.shape[1]
    return pl.pallas_call(
        _proj_residual_kernel,
        grid=(t // tm,),
        in_specs=[pl.BlockSpec((tm, d), lambda i: (i, 0)),
                  pl.BlockSpec((tm, k), lambda i: (i, 0)), _const_spec((k, d))],
        out_specs=pl.BlockSpec((tm, d), lambda i: (i, 0)),
        out_shape=jax.ShapeDtypeStruct((t, d), F32),
        compiler_params=_cparams(1), name="proj_residual",
    )(x, a, w)


def _gla_tables():
    c, s = GLA_CHUNK, GLA_SUB
    i = np.arange(c)[:, None]
    j = np.arange(c)[None, :]
    lo = (i // s) * s
    hi = lo + s
    a1 = (j >= lo) & (j <= i)
    a2 = (j > i) & (j < hi)
    a3 = j < lo
    a4 = j >= hi
    amat = np.concatenate([a1, a2, a3, a4], axis=0).astype(np.float32)
    row = np.arange(s * LANE)[:, None] // LANE
    erep = (row == (np.arange(c)[None, :] % s)).astype(np.float32)
    return jnp.asarray(amat), jnp.asarray(erep, dtype=BF16)


def _gla_kernel(q_ref, k_ref, v_ref, g_ref, la_ref, gn_ref, amat_ref, erep_ref,
                og_ref, state_s, kf_s, dq_s, tm_s, *, qscale):
    c, s = GLA_CHUNK, GLA_SUB
    nsub = c // s
    dk = q_ref.shape[1]

    @pl.when(pl.program_id(2) == 0)
    def _():
        state_s[...] = jnp.zeros_like(state_s)

    il = lax.broadcasted_iota(I32, (s, dk), 0)
    ri = lax.broadcasted_iota(I32, (c, c), 0) // s
    rj = lax.broadcasted_iota(I32, (c, c), 1) // s
    rowblk = lax.broadcasted_iota(I32, (c, dk), 0) // s
    ones_ck = jnp.ones((c, dk), F32)

    def chunk(ci, carry):
        r0 = pl.multiple_of(ci * c, c)
        la = la_ref[pl.ds(r0, c), :]
        qf = q_ref[pl.ds(r0, c), :].astype(F32) * qscale
        kf = k_ref[pl.ds(r0, c), :].astype(F32)
        vb = v_ref[pl.ds(r0, c), :]
        cum = jnp.dot(amat_ref[...], la, precision=HIGHEST, preferred_element_type=F32)
        dq, dkk, cprev, ctail = cum[0:c], cum[c:2 * c], cum[2 * c:3 * c], cum[3 * c:4 * c]
        cnext = cprev + dq + dkk
        qt = qf * jnp.exp(dq)
        kt = kf * jnp.exp(dkk)
        kf_s[...] = kf
        dq_s[...] = dq
        for r in range(nsub):
            qb = qf[r * s:(r + 1) * s]
            dqb = dq[r * s:(r + 1) * s]
            for jl in range(s):
                krow = kf_s[pl.ds(r * s + jl, 1), :]
                drow = dq_s[pl.ds(r * s + jl, 1), :]
                dec = jnp.where(il >= jl, jnp.exp(jnp.minimum(dqb - drow, 0.0)), 0.0)
                tm_s[r * s:(r + 1) * s, jl * dk:(jl + 1) * dk] = (qb * krow * dec).astype(BF16)
        sdiag = jnp.dot(tm_s[...], erep_ref[...], preferred_element_type=F32)
        soff = jnp.zeros((c, c), F32)
        for rp in range(nsub - 1):
            cn = cnext[rp * s:rp * s + 1, :]
            p = (qt * jnp.exp(jnp.minimum(cprev - cn, 0.0))).astype(BF16)
            kp = jnp.where(rowblk == rp, kt, 0.0).astype(BF16)
            soff = soff + lax.dot_general(p, kp, (((1,), (1,)), ((), ())),
                                          preferred_element_type=F32)
        sc = jnp.where(ri == rj, sdiag, jnp.where(ri > rj, soff, 0.0))
        st = state_s[...]
        o = jnp.dot(sc.astype(BF16), vb, preferred_element_type=F32) + jnp.dot(
            (qt * jnp.exp(cprev)).astype(BF16), st.astype(BF16), preferred_element_type=F32)
        ks_t = (kt * jnp.exp(ctail)).T.astype(BF16)
        upd = jnp.dot(ks_t, vb, preferred_element_type=F32)
        dcol = jnp.exp(jnp.dot(la.T, ones_ck, precision=HIGHEST, preferred_element_type=F32))
        state_s[...] = jnp.concatenate([dcol] * (st.shape[1] // dk), axis=1) * st + upd
        gf = g_ref[pl.ds(r0, c), :].astype(F32)
        og_ref[pl.ds(r0, c), :] = (_rms(o, gn_ref[...]) * (gf * _sigmoid(gf))).astype(og_ref.dtype)
        return carry

    lax.fori_loop(0, q_ref.shape[0] // c, chunk, 0)


def gla_core(qkvg, la, gnorm, batch, seq, rows=GLA_ROWS):
    t = qkvg.shape[0]
    dkt = la.shape[1]
    dk = dkt // GLA_HEADS
    dvt = (qkvg.shape[1] - 2 * dkt) // 2
    dv = dvt // GLA_HEADS
    nblk = seq // rows
    amat, erep = _gla_tables()
    row = lambda b, h, c: b * nblk + c
    return pl.pallas_call(
        functools.partial(_gla_kernel, qscale=float(dk) ** -0.5),
        grid=(batch, GLA_HEADS, nblk),
        in_specs=[pl.BlockSpec((rows, dk), lambda b, h, c: (row(b, h, c), h)),
                  pl.BlockSpec((rows, dk), lambda b, h, c: (row(b, h, c), GLA_HEADS + h)),
                  pl.BlockSpec((rows, dv), lambda b, h, c: (row(b, h, c), 2 * dkt // dv + h)),
                  pl.BlockSpec((rows, dv), lambda b, h, c: (row(b, h, c), (2 * dkt + dvt) // dv + h)),
                  pl.BlockSpec((rows, dk), lambda b, h, c: (row(b, h, c), h)),
                  _const_spec((1, dv)), _const_spec(amat.shape), _const_spec(erep.shape)],
        out_specs=pl.BlockSpec((rows, dv), lambda b, h, c: (row(b, h, c), h)),
        out_shape=jax.ShapeDtypeStruct((t, dvt), BF16),
        scratch_shapes=[pltpu.VMEM((dk, dv), F32), pltpu.VMEM((GLA_CHUNK, dk), F32),
                        pltpu.VMEM((GLA_CHUNK, dk), F32),
                        pltpu.VMEM((GLA_CHUNK, GLA_SUB * dk), BF16)],
        compiler_params=_cparams(3), name="gla_core",
    )(qkvg, qkvg, qkvg, qkvg, la, gnorm.reshape(1, dv), amat, erep)


def _sba_kernel(q_ref, k_ref, v_ref, u_ref, o_ref, acc_s, carry_s, *, scale, dh):
    bq, bk = SBA_BQ, SBA_BK
    i = pl.program_id(2)
    lane = lax.broadcasted_iota(I32, (bq, LANE), 1)
    q2 = q_ref[...].astype(F32) * scale
    qh = [jnp.where(lane < dh, q2, 0.0).astype(BF16), jnp.where(lane >= dh, q2, 0.0).astype(BF16)]
    acc_s[...] = jnp.zeros_like(acc_s)
    carry_s[...] = jnp.zeros_like(carry_s)
    qpos = i * bq + lax.broadcasted_iota(I32, (bq, bk), 0)
    kcol = lax.broadcasted_iota(I32, (bq, bk), 1)

    def cond(st):
        jb, dead = st
        return jnp.logical_and(jb >= 0, jnp.logical_not(dead))

    def body(st):
        jb, _ = st
        k0 = pl.multiple_of(jb * bk, bk)
        kb = k_ref[pl.ds(k0, bk), :]
        vb = v_ref[pl.ds(k0, bk), :]
        causal = (k0 + kcol) < qpos
        top = jnp.full((), -jnp.inf, F32)
        for hh in range(2):
            z = lax.dot_general(qh[hh], kb, (((1,), (1,)), ((), ())), preferred_element_type=F32)
            sp = jnp.maximum(z, 0.0) + jnp.log(1.0 + jnp.exp(-jnp.abs(z)))
            lf = jnp.where(causal, -sp, 0.0)
            ct = jnp.dot(lf.astype(BF16), u_ref[...], preferred_element_type=F32)
            carry = carry_s[hh]
            w = jnp.where(causal, jnp.exp(z + lf + ct[:, :bk] + carry), 0.0)
            acc_s[hh] += jnp.dot(w.astype(BF16), vb, preferred_element_type=F32)
            carry = carry + ct[:, bk:]
            carry_s[hh] = carry
            top = jnp.maximum(top, jnp.max(carry))
        return jb - 1, top < SBA_DEAD

    lax.while_loop(cond, body, ((i + 1) * (bq // bk) - 1, jnp.bool_(False)))
    o_ref[...] = jnp.where(lane < dh, acc_s[0], acc_s[1]).astype(o_ref.dtype)


def sba_core(qkv, batch, seq):
    t = qkv.shape[0]
    d = qkv.shape[1] // 3
    dh = d // SBA_HEADS
    npair = d // LANE
    bq, bk = SBA_BQ, SBA_BK
    nq = seq // bq
    u = np.concatenate([np.tril(np.ones((bk, bk), np.float32), -1), np.ones((bk, bk), np.float32)], 1)
    return pl.pallas_call(
        functools.partial(_sba_kernel, scale=float(dh) ** -0.5, dh=dh),
        grid=(batch, npair, nq),
        in_specs=[pl.BlockSpec((bq, LANE), lambda b, p, i: (b * nq + i, p)),
                  pl.BlockSpec((seq, LANE), lambda b, p, i: (b, npair + p)),
                  pl.BlockSpec((seq, LANE), lambda b, p, i: (b, 2 * npair + p)),
                  _const_spec((bk, 2 * bk))],
        out_specs=pl.BlockSpec((bq, LANE), lambda b, p, i: (b * nq + i, p)),
        out_shape=jax.ShapeDtypeStruct((t, d), BF16),
        scratch_shapes=[pltpu.VMEM((2, bq, LANE), F32), pltpu.VMEM((2, bq, bk), F32)],
        compiler_params=_cparams(3), name="sba_core",
    )(qkv, qkv, qkv, jnp.asarray(u, dtype=BF16))


def _pool_kernel(x_ref, halo_ref, g_ref, win_ref, wgrp_ref, sc_ref, o_ref, u_s, *, tiles_per_seq):
    tm, d = x_ref.shape
    hal = halo_ref.shape[0]
    gd = wgrp_ref.shape[1]
    first = (pl.program_id(0) % tiles_per_seq) == 0
    x = x_ref[...]
    xx = jnp.concatenate([halo_ref[...], x], axis=0)
    h = _rms(xx, g_ref[...]).astype(BF16)
    u = jnp.dot(h, win_ref[...], preferred_element_type=F32)
    keep = jnp.where(first, 0.0, 1.0)
    rowi = lax.broadcasted_iota(I32, (hal + tm, 1), 0)
    u_s[...] = u * jnp.where(rowi < hal, keep, 1.0)
    pos = (pl.program_id(0) % tiles_per_seq) * tm + lax.broadcasted_iota(I32, (tm, 1), 0)
    mixed = []
    for gi, w in enumerate(POOL_WINDOWS):
        cs = slice(gi * gd, (gi + 1) * gd)
        ssum = u_s[hal:hal + tm, cs]
        for dlt in range(1, w):
            ssum = ssum + u_s[hal - dlt:hal - dlt + tm, cs]
        cnt = jnp.minimum(pos + 1, w).astype(F32)
        pooled = ssum / cnt - u_s[hal:hal + tm, cs]
        mixed.append(jnp.dot(pooled.astype(BF16), wgrp_ref[gi], preferred_element_type=F32))
    o_ref[...] = x + jnp.concatenate(mixed, axis=1) * sc_ref[...]


def pool_mixer(x, gain, w_in, w_group, scale, seq, tm=ROW_TILE):
    t, d = x.shape
    hal = max(POOL_WINDOWS)
    g, gd, _ = w_group.shape
    return pl.pallas_call(
        functools.partial(_pool_kernel, tiles_per_seq=seq // tm),
        grid=(t // tm,),
        in_specs=[pl.BlockSpec((tm, d), lambda i: (i, 0)),
                  pl.BlockSpec((hal, d), lambda i: (jnp.maximum(i * (tm // hal) - 1, 0), 0)),
                  _const_spec((1, d)), _const_spec((d, d)), _const_spec((g, gd, gd)),
                  _const_spec((1, d))],
        out_specs=pl.BlockSpec((tm, d), lambda i: (i, 0)),
        out_shape=jax.ShapeDtypeStruct((t, d), F32),
        scratch_shapes=[pltpu.VMEM((hal + tm, d), F32)],
        compiler_params=_cparams(1), name="pool_mixer",
    )(x, x, gain.reshape(1, d), w_in, w_group, scale.reshape(1, d))


def _swiglu_acc(h, wg_ref, wu_ref, wd_ref, acc, widx):
    ff = wg_ref.shape[-1]
    for c0 in range(0, ff, FF_CHUNK):
        cs = slice(c0, c0 + FF_CHUNK)
        gate = jnp.dot(h, wg_ref[widx + (slice(None), cs)], preferred_element_type=F32)
        up = jnp.dot(h, wu_ref[widx + (slice(None), cs)], preferred_element_type=F32)
        a = (gate * _sigmoid(gate) * up).astype(BF16)
        acc = acc + jnp.dot(a, wd_ref[widx + (cs, slice(None))], preferred_element_type=F32)
    return acc


def _ffn_kernel(x_ref, g_ref, wg_ref, wu_ref, wd_ref, o_ref):
    x = x_ref[...]
    h = _rms(x, g_ref[...]).astype(BF16)
    o_ref[...] = _swiglu_acc(h, wg_ref, wu_ref, wd_ref, x, ())


def ffn_dense(x, gain, wg, wu, wd, tm=ROW_TILE):
    t, d = x.shape
    ff = wg.shape[1]
    return pl.pallas_call(
        _ffn_kernel,
        grid=(t // tm,),
        in_specs=[pl.BlockSpec((tm, d), lambda i: (i, 0)), _const_spec((1, d)),
                  _const_spec((d, ff)), _const_spec((d, ff)), _const_spec((ff, d))],
        out_specs=pl.BlockSpec((tm, d), lambda i: (i, 0)),
        out_shape=jax.ShapeDtypeStruct((t, d), F32),
        compiler_params=_cparams(1), name="ffn_dense",
    )(x, gain.reshape(1, d), wg, wu, wd)


def _moe_ffn_kernel(texp_ref, nused_ref, x_ref, g_ref, wg_ref, wu_ref, wd_ref, o_ref):
    live = pl.program_id(0) < nused_ref[0]

    @pl.when(live)
    def _():
        h = _rms(x_ref[...], g_ref[...]).astype(BF16)
        o_ref[...] = _swiglu_acc(h, wg_ref, wu_ref, wd_ref, jnp.zeros(o_ref.shape, F32), (0,))

    @pl.when(jnp.logical_not(live))
    def _():
        o_ref[...] = jnp.zeros_like(o_ref)


def moe_ffn(xs, gain, wg, wu, wd, tile_expert, n_used, tm):
    r, d = xs.shape
    ff = wg.shape[2]
    live = lambda i, te, nu: jnp.minimum(i, nu[0] - 1)
    grid_spec = pltpu.PrefetchScalarGridSpec(
        num_scalar_prefetch=2,
        grid=(r // tm,),
        in_specs=[pl.BlockSpec((tm, d), lambda i, te, nu: (live(i, te, nu), 0)),
                  pl.BlockSpec((1, d), lambda i, te, nu: (0, 0)),
                  pl.BlockSpec((1, d, ff), lambda i, te, nu: (te[i], 0, 0)),
                  pl.BlockSpec((1, d, ff), lambda i, te, nu: (te[i], 0, 0)),
                  pl.BlockSpec((1, ff, d), lambda i, te, nu: (te[i], 0, 0))],
        out_specs=pl.BlockSpec((tm, d), lambda i, te, nu: (i, 0)),
    )
    return pl.pallas_call(
        _moe_ffn_kernel, grid_spec=grid_spec,
        out_shape=jax.ShapeDtypeStruct((r, d), F32),
        compiler_params=_cparams(1), name="moe_ffn",
    )(tile_expert, n_used, xs, gain.reshape(1, d), wg, wu, wd)


def _router_kernel(x_ref, g_ref, wr_ref, idx_ref, prob_ref, *, ne):
    h = _rms(x_ref[...], g_ref[...])
    logits = jnp.dot(h, wr_ref[...], precision=HIGHEST, preferred_element_type=F32)
    col = lax.broadcasted_iota(I32, logits.shape, 1).astype(F32)
    logits = jnp.where(col < ne, logits, -jnp.inf)
    m1 = jnp.max(logits, axis=1, keepdims=True)
    i1 = jnp.min(jnp.where(logits == m1, col, float(ne)), axis=1, keepdims=True)
    rest = jnp.where(col == i1, -jnp.inf, logits)
    m2 = jnp.max(rest, axis=1, keepdims=True)
    i2 = jnp.min(jnp.where(rest == m2, col, float(ne)), axis=1, keepdims=True)
    e2 = jnp.exp(m2 - m1)
    p1 = 1.0 / (1.0 + e2)
    idx_ref[...] = jnp.where(col == 0.0, i1, i2).astype(I32)
    prob_ref[...] = jnp.where(col == 0.0, p1, e2 * p1)


def router_top2(x, gain, wr, tm=ROW_TILE):
    t, d = x.shape
    ne = wr.shape[1]
    wr_pad = jnp.zeros((d, LANE), F32).at[:, :ne].set(wr)
    return pl.pallas_call(
        functools.partial(_router_kernel, ne=ne),
        grid=(t // tm,),
        in_specs=[pl.BlockSpec((tm, d), lambda i: (i, 0)), _const_spec((1, d)), _const_spec((d, LANE))],
        out_specs=[pl.BlockSpec((tm, LANE), lambda i: (i, 0)), pl.BlockSpec((tm, LANE), lambda i: (i, 0))],
        out_shape=[jax.ShapeDtypeStruct((t, LANE), I32), jax.ShapeDtypeStruct((t, LANE), F32)],
        compiler_params=_cparams(1), name="router_top2",
    )(x, gain.reshape(1, d), wr_pad)


def _row_copy(src_hbm, dst_hbm, src_row, dst_row, sem):
    return pltpu.make_async_copy(src_hbm.at[pl.ds(src_row, 1)], dst_hbm.at[pl.ds(dst_row, 1)], sem)


def _gather_kernel(idx_ref, x_hbm, o_hbm, sem, *, rows):
    base = pl.program_id(0) * rows

    def issue(k, c):
        _row_copy(x_hbm, o_hbm, idx_ref[0, 0, k], base + k, sem).start()
        return c

    lax.fori_loop(0, rows, issue, 0)
    pltpu.make_async_copy(x_hbm.at[pl.ds(0, rows)], o_hbm.at[pl.ds(base, rows)], sem).wait()


def gather_rows(x, src, rows=GATHER_ROWS):
    r = src.shape[0]
    d = x.shape[1]
    return pl.pallas_call(
        functools.partial(_gather_kernel, rows=rows),
        grid=(r // rows,),
        in_specs=[pl.BlockSpec((1, 1, rows), lambda i: (i, 0, 0), memory_space=pltpu.SMEM),
                  pl.BlockSpec(memory_space=pl.ANY)],
        out_specs=pl.BlockSpec(memory_space=pl.ANY),
        out_shape=jax.ShapeDtypeStruct((r, d), x.dtype),
        scratch_shapes=[pltpu.SemaphoreType.DMA(())],
        compiler_params=_cparams(1), name="gather_rows",
    )(src.reshape(r // rows, 1, rows), x)


def _combine_kernel(pos_ref, x_ref, p_ref, g_ref, y_hbm, o_ref, buf, sem, *, final_norm):
    tm = x_ref.shape[0]

    def issue(k, c):
        pltpu.make_async_copy(y_hbm.at[pl.ds(pos_ref[0, 0, k], 1)], buf.at[0, pl.ds(k, 1)], sem).start()
        pltpu.make_async_copy(y_hbm.at[pl.ds(pos_ref[0, 1, k], 1)], buf.at[1, pl.ds(k, 1)], sem).start()
        return c

    lax.fori_loop(0, tm, issue, 0)
    pltpu.make_async_copy(y_hbm.at[pl.ds(0, tm)], buf.at[0], sem).wait()
    pltpu.make_async_copy(y_hbm.at[pl.ds(0, tm)], buf.at[1], sem).wait()
    p = p_ref[...]
    out = x_ref[...] + p[:, 0:1] * buf[0] + p[:, 1:2] * buf[1]
    if final_norm:
        out = _rms(out, g_ref[...])
    o_ref[...] = out


def moe_combine(x, ys, pos, prob, final_gain, tm=256):
    t, d = x.shape
    final_norm = final_gain is not None
    gain = final_gain if final_norm else jnp.ones((d,), F32)
    pos3 = pos.reshape(t // tm, tm, 2).transpose(0, 2, 1)
    return pl.pallas_call(
        functools.partial(_combine_kernel, final_norm=final_norm),
        grid=(t // tm,),
        in_specs=[pl.BlockSpec((1, 2, tm), lambda i: (i, 0, 0), memory_space=pltpu.SMEM),
                  pl.BlockSpec((tm, d), lambda i: (i, 0)),
                  pl.BlockSpec((tm, LANE), lambda i: (i, 0)),
                  _const_spec((1, d)),
                  pl.BlockSpec(memory_space=pl.ANY)],
        out_specs=pl.BlockSpec((tm, d), lambda i: (i, 0)),
        out_shape=jax.ShapeDtypeStruct((t, d), F32),
        scratch_shapes=[pltpu.VMEM((2, tm, d), F32), pltpu.SemaphoreType.DMA(())],
        compiler_params=_cparams(1), name="moe_combine",
    )(pos3, x, prob, gain.reshape(1, d), ys)


def _final_norm_kernel(x_ref, g_ref, o_ref):
    o_ref[...] = _rms(x_ref[...], g_ref[...])


def final_norm(x, gain, tm=ROW_TILE):
    t, d = x.shape
    return pl.pallas_call(
        _final_norm_kernel, grid=(t // tm,),
        in_specs=[pl.BlockSpec((tm, d), lambda i: (i, 0)), _const_spec((1, d))],
        out_specs=pl.BlockSpec((tm, d), lambda i: (i, 0)),
        out_shape=jax.ShapeDtypeStruct((t, d), F32),
        compiler_params=_cparams(1), name="final_norm",
    )(x, gain.reshape(1, d))


def _route_plan(idx, tm):
    t = idx.shape[0]
    e_flat = idx.reshape(-1)
    onehot = (e_flat[:, None] == jnp.arange(N_EXPERTS, dtype=I32)[None, :]).astype(I32)
    csum = jnp.cumsum(onehot, axis=0)
    rank = jnp.take_along_axis(csum, e_flat[:, None], axis=1)[:, 0] - 1
    counts = csum[-1]
    padded = ((counts + tm - 1) // tm) * tm
    ends = jnp.cumsum(padded)
    pos = (ends - padded)[e_flat] + rank
    n_rows = 2 * t + N_EXPERTS * tm
    src = jnp.zeros((n_rows,), I32).at[pos].set(jnp.arange(2 * t, dtype=I32) // 2)
    n_used = (ends[-1] // tm).astype(I32)
    tile_start = jnp.arange(n_rows // tm, dtype=I32) * tm
    tile_expert = jnp.searchsorted(ends, jnp.minimum(tile_start, ends[-1] - 1), side="right")
    tile_expert = jnp.minimum(tile_expert, N_EXPERTS - 1).astype(I32)
    return src, pos.reshape(t, 2), tile_expert, n_used.reshape(1)


def moe_layer(x, gain, router, wg, wu, wd, final_gain, tm=MOE_TILE, tc=256):
    idx, prob = router_top2(x, gain, router)
    src, pos, tile_expert, n_used = _route_plan(idx[:, :2], tm)
    xs = gather_rows(x, src, rows=min(GATHER_ROWS, tm))
    ys = moe_ffn(xs, gain, wg, wu, wd, tile_expert, n_used, tm)
    return moe_combine(x, ys, pos, prob, final_gain, tm=tc)


def gla_layer(x, gain, w_in, w_gate_up, b_gate, gnorm, w_out, batch, seq):
    d = x.shape[1]
    n_main = w_in.shape[1] - GLA_GATE_RANK
    w_main = w_in[:, :n_main].astype(BF16)
    w_a = jnp.zeros((d, LANE), F32).at[:, :GLA_GATE_RANK].set(w_in[:, n_main:])
    w_gu = jnp.zeros((LANE, w_gate_up.shape[1]), F32).at[:GLA_GATE_RANK].set(w_gate_up)
    qkvg, la = gla_proj(x, gain, w_main, w_a, w_gu, b_gate)
    og = gla_core(qkvg, la, gnorm, batch, seq, rows=min(GLA_ROWS, seq))
    return proj_residual(x, og, w_out.astype(BF16))


def sba_layer(x, gain, w_in, w_out, batch, seq):
    qkv = norm_proj(x, gain, w_in.astype(BF16))
    o = sba_core(qkv, batch, seq)
    return proj_residual(x, o, w_out.astype(BF16))


def kernel(x, norm_mix, norm_ffn, norm_final, gla_w_in, gla_w_gate_up, gla_b_gate, gla_norm, gla_w_out, sba_w_in, sba_w_out, pool_w_in, pool_w_group, pool_scale, ffn_w_gate, ffn_w_up, ffn_w_down, moe_router, moe_w_gate, moe_w_up, moe_w_down):
    batch, seq, d = x.shape
    depth = norm_mix.shape[0]
    x = x.reshape(batch * seq, d)
    normed = False
    for i in range(depth):
        kind, j = i % 3, i // 3
        if kind == 0:
            x = gla_layer(x, norm_mix[i], gla_w_in[j], gla_w_gate_up[j], gla_b_gate[j],
                          gla_norm[j], gla_w_out[j], batch, seq)
        elif kind == 1:
            x = sba_layer(x, norm_mix[i], sba_w_in[j], sba_w_out[j], batch, seq)
        else:
            x = pool_mixer(x, norm_mix[i], pool_w_in[j].astype(BF16), pool_w_group[j].astype(BF16),
                           pool_scale[j], seq)
        f = i // 2
        if i % 2 == 0:
            x = ffn_dense(x, norm_ffn[i], ffn_w_gate[f].astype(BF16), ffn_w_up[f].astype(BF16),
                          ffn_w_down[f].astype(BF16))
        else:
            last = i == depth - 1
            x = moe_layer(x, norm_ffn[i], moe_router[f], moe_w_gate[f].astype(BF16),
                          moe_w_up[f].astype(BF16), moe_w_down[f].astype(BF16),
                          norm_final if last else None)
            normed = last
    if not normed:
        x = final_norm(x, norm_final)
    return x.reshape(batch, seq, d)
```

```python
import functools

import numpy as np
import jax
import jax.numpy as jnp
from jax import lax
from jax.experimental import pallas as pl
from jax.experimental.pallas import tpu as pltpu

F32 = jnp.float32
BF16 = jnp.bfloat16
I32 = jnp.int32
HIGHEST = lax.Precision.HIGHEST

RMS_EPS = 1e-6
GLA_HEADS = 4
GLA_GATE_RANK = 16
GLA_GATE_NORMALIZER = 16.0
SBA_HEADS = 16
POOL_WINDOWS = (2, 4, 8, 16)
N_EXPERTS = 8

LANE = 128
VMEM_LIMIT = 56 * 1024 * 1024
ROW_TILE = 512
FF_CHUNK = 256
GLA_CHUNK = 64
GLA_SUB = 16
GLA_ROWS = 512
SBA_BQ = 256
SBA_ROWS = 1024
SBA_DEAD2 = 150.0
LOG2E = 1.4426950408889634
MOE_TILE = 512
GATHER_ROWS = 512


def _cparams(n_axes):
    return pltpu.CompilerParams(dimension_semantics=("arbitrary",) * n_axes,
                                vmem_limit_bytes=VMEM_LIMIT)


def _rms(xf, gain):
    return xf * lax.rsqrt(jnp.mean(xf * xf, axis=-1, keepdims=True) + RMS_EPS) * gain


def _sigmoid(x):
    return 1.0 / (1.0 + jnp.exp(-x))


def _log_sigmoid(z):
    return jnp.minimum(z, 0.0) - jnp.log(1.0 + jnp.exp(-jnp.abs(z)))


def _const_spec(shape):
    return pl.BlockSpec(shape, lambda *_: (0,) * len(shape))


def _norm_proj_kernel(x_ref, g_ref, w_ref, o_ref, *, nchunk):
    h = _rms(x_ref[...], g_ref[...]).astype(BF16)
    n = o_ref.shape[1]
    for c0 in range(0, n, nchunk):
        o_ref[:, c0:c0 + nchunk] = jnp.dot(
            h, w_ref[:, c0:c0 + nchunk], preferred_element_type=F32).astype(o_ref.dtype)


def norm_proj(x, gain, w, tm=ROW_TILE):
    t, d = x.shape
    n = w.shape[1]
    return pl.pallas_call(
        functools.partial(_norm_proj_kernel, nchunk=512),
        grid=(t // tm,),
        in_specs=[pl.BlockSpec((tm, d), lambda i: (i, 0)),
                  _const_spec((1, d)), _const_spec((d, n))],
        out_specs=pl.BlockSpec((tm, n), lambda i: (i, 0)),
        out_shape=jax.ShapeDtypeStruct((t, n), BF16),
        compiler_params=_cparams(1), name="norm_proj",
    )(x, gain.reshape(1, d), w)


def _gla_proj_kernel(x_ref, g_ref, w_ref, wa_ref, wgu_ref, bg_ref, o_ref, la_ref, *, nchunk):
    hf = _rms(x_ref[...], g_ref[...])
    h = hf.astype(BF16)
    n = o_ref.shape[1]
    for c0 in range(0, n, nchunk):
        o_ref[:, c0:c0 + nchunk] = jnp.dot(
            h, w_ref[:, c0:c0 + nchunk], preferred_element_type=F32).astype(o_ref.dtype)
    a = jnp.dot(hf, wa_ref[...], precision=HIGHEST, preferred_element_type=F32)
    z = jnp.dot(a, wgu_ref[...], precision=HIGHEST, preferred_element_type=F32) + bg_ref[...]
    la_ref[...] = _log_sigmoid(z) / GLA_GATE_NORMALIZER


def gla_proj(x, gain, w_main, w_a, w_gu, b_gate, tm=ROW_TILE):
    t, d = x.shape
    n = w_main.shape[1]
    dk = w_gu.shape[1]
    return pl.pallas_call(
        functools.partial(_gla_proj_kernel, nchunk=512),
        grid=(t // tm,),
        in_specs=[pl.BlockSpec((tm, d), lambda i: (i, 0)),
                  _const_spec((1, d)), _const_spec((d, n)),
                  _const_spec((d, LANE)), _const_spec((LANE, dk)), _const_spec((1, dk))],
        out_specs=[pl.BlockSpec((tm, n), lambda i: (i, 0)),
                   pl.BlockSpec((tm, dk), lambda i: (i, 0))],
        out_shape=[jax.ShapeDtypeStruct((t, n), BF16), jax.ShapeDtypeStruct((t, dk), F32)],
        compiler_params=_cparams(1), name="gla_proj",
    )(x, gain.reshape(1, d), w_main, w_a, w_gu, b_gate.reshape(1, dk))


def _proj_residual_kernel(x_ref, a_ref, w_ref, o_ref):
    o_ref[...] = x_ref[...] + jnp.dot(a_ref[...], w_ref[...], preferred_element_type=F32)


def proj_residual(x, a, w, tm=ROW_TILE):
    t, d = x.shape
    k = a.shape[1]
    return pl.pallas_call(
        _proj_residual_kernel,
        grid=(t // tm,),
        in_specs=[pl.BlockSpec((tm, d), lambda i: (i, 0)),
                  pl.BlockSpec((tm, k), lambda i: (i, 0)), _const_spec((k, d))],
        out_specs=pl.BlockSpec((tm, d), lambda i: (i, 0)),
        out_shape=jax.ShapeDtypeStruct((t, d), F32),
        compiler_params=_cparams(1), name="proj_residual",
    )(x, a, w)


def _gla_tables():
    c, s = GLA_CHUNK, GLA_SUB
    i = np.arange(c)[:, None]
    j = np.arange(c)[None, :]
    lo = (i // s) * s
    hi = lo + s
    a1 = (j >= lo) & (j <= i)
    a2 = (j > i) & (j < hi)
    a3 = j < lo
    a4 = j >= hi
    amat = np.concatenate([a1, a2, a3, a4], axis=0).astype(np.float32)
    row = np.arange(s * LANE)[:, None] // LANE
    erep = (row == (np.arange(c)[None, :] % s)).astype(np.float32)
    return jnp.asarray(amat), jnp.asarray(erep, dtype=BF16)


def _gla_kernel(q_ref, k_ref, v_ref, g_ref, la_ref, gn_ref, amat_ref, erep_ref,
                og_ref, state_s, kf_s, dq_s, tm_s, *, qscale):
    c, s = GLA_CHUNK, GLA_SUB
    nsub = c // s
    nh = state_s.shape[0]
    dk = q_ref.shape[1] // nh
    dv = v_ref.shape[1] // nh

    @pl.when(pl.program_id(1) == 0)
    def _():
        state_s[...] = jnp.zeros_like(state_s)

    il = lax.broadcasted_iota(I32, (s, dk), 0)
    ri = lax.broadcasted_iota(I32, (c, c), 0) // s
    rj = lax.broadcasted_iota(I32, (c, c), 1) // s
    rowblk = lax.broadcasted_iota(I32, (c, dk), 0) // s
    ones_ck = jnp.ones((c, dk), F32)

    def head_chunk(h, r0):
        kcols = slice(h * dk, (h + 1) * dk)
        vcols = slice(h * dv, (h + 1) * dv)
        la = la_ref[pl.ds(r0, c), kcols]
        qf = q_ref[pl.ds(r0, c), kcols].astype(F32) * qscale
        kf = k_ref[pl.ds(r0, c), kcols].astype(F32)
        vb = v_ref[pl.ds(r0, c), vcols]
        cum = jnp.dot(amat_ref[...], la, precision=HIGHEST, preferred_element_type=F32)
        dq, dkk, cprev, ctail = cum[0:c], cum[c:2 * c], cum[2 * c:3 * c], cum[3 * c:4 * c]
        cnext = cprev + dq + dkk
        qt = qf * jnp.exp(dq)
        kt = kf * jnp.exp(dkk)
        kf_s[h] = kf
        dq_s[h] = dq
        for r in range(nsub):
            qb = qf[r * s:(r + 1) * s]
            dqb = dq[r * s:(r + 1) * s]
            for jl in range(s):
                krow = kf_s[h, pl.ds(r * s + jl, 1), :]
                drow = dq_s[h, pl.ds(r * s + jl, 1), :]
                dec = jnp.where(il >= jl, jnp.exp(jnp.minimum(dqb - drow, 0.0)), 0.0)
                tm_s[h, r * s:(r + 1) * s, jl * dk:(jl + 1) * dk] = (qb * krow * dec).astype(BF16)
        sdiag = jnp.dot(tm_s[h], erep_ref[...], preferred_element_type=F32)
        soff = jnp.zeros((c, c), F32)
        for rp in range(nsub - 1):
            cn = cnext[rp * s:rp * s + 1, :]
            p = (qt * jnp.exp(jnp.minimum(cprev - cn, 0.0))).astype(BF16)
            kp = jnp.where(rowblk == rp, kt, 0.0).astype(BF16)
            soff = soff + lax.dot_general(p, kp, (((1,), (1,)), ((), ())),
                                          preferred_element_type=F32)
        sc = jnp.where(ri == rj, sdiag, jnp.where(ri > rj, soff, 0.0))
        st = state_s[h]
        o = jnp.dot(sc.astype(BF16), vb, preferred_element_type=F32) + jnp.dot(
            (qt * jnp.exp(cprev)).astype(BF16), st.astype(BF16), preferred_element_type=F32)
        ks_t = (kt * jnp.exp(ctail)).T.astype(BF16)
        upd = jnp.dot(ks_t, vb, preferred_element_type=F32)
        dcol = jnp.exp(jnp.dot(la.T, ones_ck, precision=HIGHEST, preferred_element_type=F32))
        state_s[h] = jnp.concatenate([dcol] * (dv // dk), axis=1) * st + upd
        gf = g_ref[pl.ds(r0, c), vcols].astype(F32)
        og_ref[pl.ds(r0, c), vcols] = (_rms(o, gn_ref[...]) * (gf * _sigmoid(gf))).astype(og_ref.dtype)

    def chunk(ci, carry):
        r0 = pl.multiple_of(ci * c, c)
        for h in range(nh):
            head_chunk(h, r0)
        return carry

    lax.fori_loop(0, q_ref.shape[0] // c, chunk, 0)


def gla_core(qkvg, la, gnorm, batch, seq, rows=GLA_ROWS):
    t = qkvg.shape[0]
    dkt = la.shape[1]
    dk = dkt // GLA_HEADS
    dvt = (qkvg.shape[1] - 2 * dkt) // 2
    dv = dvt // GLA_HEADS
    nblk = seq // rows
    amat, erep = _gla_tables()
    row = lambda b, c: b * nblk + c
    return pl.pallas_call(
        functools.partial(_gla_kernel, qscale=float(dk) ** -0.5),
        grid=(batch, nblk),
        in_specs=[pl.BlockSpec((rows, dkt), lambda b, c: (row(b, c), 0)),
                  pl.BlockSpec((rows, dkt), lambda b, c: (row(b, c), 1)),
                  pl.BlockSpec((rows, dvt), lambda b, c: (row(b, c), 2 * dkt // dvt)),
                  pl.BlockSpec((rows, dvt), lambda b, c: (row(b, c), 2 * dkt // dvt + 1)),
                  pl.BlockSpec((rows, dkt), lambda b, c: (row(b, c), 0)),
                  _const_spec((1, dv)), _const_spec(amat.shape), _const_spec(erep.shape)],
        out_specs=pl.BlockSpec((rows, dvt), lambda b, c: (row(b, c), 0)),
        out_shape=jax.ShapeDtypeStruct((t, dvt), BF16),
        scratch_shapes=[pltpu.VMEM((GLA_HEADS, dk, dv), F32),
                        pltpu.VMEM((GLA_HEADS, GLA_CHUNK, dk), F32),
                        pltpu.VMEM((GLA_HEADS, GLA_CHUNK, dk), F32),
                        pltpu.VMEM((GLA_HEADS, GLA_CHUNK, GLA_SUB * dk), BF16)],
        compiler_params=_cparams(2), name="gla_core",
    )(qkvg, qkvg, qkvg, qkvg, la, gnorm.reshape(1, dv), amat, erep)


def _sba_kernel(q_ref, k_ref, v_ref, u_ref, o_ref, qs_s, acc_s, carry_s, *, scale, dh):
    bq = SBA_BQ
    nq = q_ref.shape[0] // bq
    step = pl.program_id(2)
    lane = lax.broadcasted_iota(I32, (bq, LANE), 1)
    row = lax.broadcasted_iota(I32, (2 * bq, bq), 0)
    strict = lax.broadcasted_iota(I32, (2 * bq, bq), 1) < jnp.where(row >= bq, row - bq, row)

    def block(jb, diagonal):
        k0 = pl.multiple_of(jb * bq, bq)
        kb = k_ref[pl.ds(k0, bq), :]
        vb = v_ref[pl.ds(k0, bq), :]
        z = lax.dot_general(qs_s[...], kb, (((1,), (1,)), ((), ())), preferred_element_type=F32)
        sp = jnp.maximum(z, 0.0) + jnp.log2(1.0 + jnp.exp2(jnp.minimum(z, -z)))
        if diagonal:
            sp = jnp.where(strict, sp, 0.0)
        ct = jnp.dot(sp.astype(BF16), u_ref[...], preferred_element_type=F32)
        carry = carry_s[...]
        w = jnp.exp2((z - sp) - ct[:, :bq] - jnp.concatenate([carry] * (bq // LANE), axis=1))
        if diagonal:
            w = jnp.where(strict, w, 0.0)
        acc_s[...] += jnp.dot(w.astype(BF16), vb, preferred_element_type=F32)
        carry = carry + ct[:, bq:]
        carry_s[...] = carry
        return jnp.min(carry)

    def qblock(qi, c):
        r0 = pl.multiple_of(qi * bq, bq)
        q2 = q_ref[pl.ds(r0, bq), :].astype(F32) * scale
        qs_s[0:bq] = jnp.where(lane < dh, q2, 0.0).astype(BF16)
        qs_s[bq:2 * bq] = jnp.where(lane >= dh, q2, 0.0).astype(BF16)
        acc_s[...] = jnp.zeros_like(acc_s)
        carry_s[...] = jnp.zeros_like(carry_s)
        jd = step * nq + qi
        low = block(jd, True)
        lax.while_loop(lambda st: jnp.logical_and(st[0] >= 0, st[1] < SBA_DEAD2),
                       lambda st: (st[0] - 1, block(st[0], False)), (jd - 1, low))
        a = acc_s[...]
        o_ref[pl.ds(r0, bq), :] = jnp.where(lane < dh, a[:bq], a[bq:]).astype(o_ref.dtype)
        return c

    lax.fori_loop(0, nq, qblock, 0)


def sba_core(qkv, batch, seq):
    t = qkv.shape[0]
    d = qkv.shape[1] // 3
    dh = d // SBA_HEADS
    npair = d // LANE
    bq = SBA_BQ
    rows = min(SBA_ROWS, seq)
    nstep = seq // rows
    u = np.concatenate([np.tril(np.ones((bq, bq), np.float32), -1), np.ones((bq, LANE), np.float32)], 1)
    return pl.pallas_call(
        functools.partial(_sba_kernel, scale=float(dh) ** -0.5 * LOG2E, dh=dh),
        grid=(batch, npair, nstep),
        in_specs=[pl.BlockSpec((rows, LANE), lambda b, p, i: (b * nstep + i, p)),
                  pl.BlockSpec((seq, LANE), lambda b, p, i: (b, npair + p)),
                  pl.BlockSpec((seq, LANE), lambda b, p, i: (b, 2 * npair + p)),
                  _const_spec((bq, bq + LANE))],
        out_specs=pl.BlockSpec((rows, LANE), lambda b, p, i: (b * nstep + i, p)),
        out_shape=jax.ShapeDtypeStruct((t, d), BF16),
        scratch_shapes=[pltpu.VMEM((2 * bq, LANE), BF16), pltpu.VMEM((2 * bq, LANE), F32),
                        pltpu.VMEM((2 * bq, LANE), F32)],
        compiler_params=_cparams(3), name="sba_core",
    )(qkv, qkv, qkv, jnp.asarray(u, dtype=BF16))


def _pool_kernel(x_ref, halo_ref, g_ref, win_ref, wgrp_ref, sc_ref, o_ref, u_s, *, tiles_per_seq):
    tm, d = x_ref.shape
    hal = halo_ref.shape[0]
    gd = wgrp_ref.shape[1]
    first = (pl.program_id(0) % tiles_per_seq) == 0
    x = x_ref[...]
    xx = jnp.concatenate([halo_ref[...], x], axis=0)
    h = _rms(xx, g_ref[...]).astype(BF16)
    u = jnp.dot(h, win_ref[...], preferred_element_type=F32)
    keep = jnp.where(first, 0.0, 1.0)
    rowi = lax.broadcasted_iota(I32, (hal + tm, 1), 0)
    u_s[...] = u * jnp.where(rowi < hal, keep, 1.0)
    pos = (pl.program_id(0) % tiles_per_seq) * tm + lax.broadcasted_iota(I32, (tm, 1), 0)
    mixed = []
    for gi, w in enumerate(POOL_WINDOWS):
        cs = slice(gi * gd, (gi + 1) * gd)
        ssum = u_s[hal:hal + tm, cs]
        for dlt in range(1, w):
            ssum = ssum + u_s[hal - dlt:hal - dlt + tm, cs]
        cnt = jnp.minimum(pos + 1, w).astype(F32)
        pooled = ssum / cnt - u_s[hal:hal + tm, cs]
        mixed.append(jnp.dot(pooled.astype(BF16), wgrp_ref[gi], preferred_element_type=F32))
    o_ref[...] = x + jnp.concatenate(mixed, axis=1) * sc_ref[...]


def pool_mixer(x, gain, w_in, w_group, scale, seq, tm=ROW_TILE):
    t, d = x.shape
    hal = max(POOL_WINDOWS)
    g, gd, _ = w_group.shape
    return pl.pallas_call(
        functools.partial(_pool_kernel, tiles_per_seq=seq // tm),
        grid=(t // tm,),
        in_specs=[pl.BlockSpec((tm, d), lambda i: (i, 0)),
                  pl.BlockSpec((hal, d), lambda i: (jnp.maximum(i * (tm // hal) - 1, 0), 0)),
                  _const_spec((1, d)), _const_spec((d, d)), _const_spec((g, gd, gd)),
                  _const_spec((1, d))],
        out_specs=pl.BlockSpec((tm, d), lambda i: (i, 0)),
        out_shape=jax.ShapeDtypeStruct((t, d), F32),
        scratch_shapes=[pltpu.VMEM((hal + tm, d), F32)],
        compiler_params=_cparams(1), name="pool_mixer",
    )(x, x, gain.reshape(1, d), w_in, w_group, scale.reshape(1, d))


def _swiglu_acc(h, wg_ref, wu_ref, wd_ref, acc, widx):
    ff = wg_ref.shape[-1]
    for c0 in range(0, ff, FF_CHUNK):
        cs = slice(c0, c0 + FF_CHUNK)
        gate = jnp.dot(h, wg_ref[widx + (slice(None), cs)], preferred_element_type=F32)
        up = jnp.dot(h, wu_ref[widx + (slice(None), cs)], preferred_element_type=F32)
        a = (gate * _sigmoid(gate) * up).astype(BF16)
        acc = acc + jnp.dot(a, wd_ref[widx + (cs, slice(None))], preferred_element_type=F32)
    return acc


def _ffn_kernel(x_ref, g_ref, wg_ref, wu_ref, wd_ref, o_ref):
    x = x_ref[...]
    h = _rms(x, g_ref[...]).astype(BF16)
    o_ref[...] = _swiglu_acc(h, wg_ref, wu_ref, wd_ref, x, ())


def ffn_dense(x, gain, wg, wu, wd, tm=ROW_TILE):
    t, d = x.shape
    ff = wg.shape[1]
    return pl.pallas_call(
        _ffn_kernel,
        grid=(t // tm,),
        in_specs=[pl.BlockSpec((tm, d), lambda i: (i, 0)), _const_spec((1, d)),
                  _const_spec((d, ff)), _const_spec((d, ff)), _const_spec((ff, d))],
        out_specs=pl.BlockSpec((tm, d), lambda i: (i, 0)),
        out_shape=jax.ShapeDtypeStruct((t, d), F32),
        compiler_params=_cparams(1), name="ffn_dense",
    )(x, gain.reshape(1, d), wg, wu, wd)


def _moe_ffn_kernel(texp_ref, nused_ref, x_ref, g_ref, wg_ref, wu_ref, wd_ref, o_ref):
    live = pl.program_id(0) < nused_ref[0]

    @pl.when(live)
    def _():
        h = _rms(x_ref[...], g_ref[...]).astype(BF16)
        o_ref[...] = _swiglu_acc(h, wg_ref, wu_ref, wd_ref, jnp.zeros(o_ref.shape, F32), (0,))

    @pl.when(jnp.logical_not(live))
    def _():
        o_ref[...] = jnp.zeros_like(o_ref)


def moe_ffn(xs, gain, wg, wu, wd, tile_expert, n_used, tm):
    r, d = xs.shape
    ff = wg.shape[2]
    live = lambda i, te, nu: jnp.minimum(i, nu[0] - 1)
    grid_spec = pltpu.PrefetchScalarGridSpec(
        num_scalar_prefetch=2,
        grid=(r // tm,),
        in_specs=[pl.BlockSpec((tm, d), lambda i, te, nu: (live(i, te, nu), 0)),
                  pl.BlockSpec((1, d), lambda i, te, nu: (0, 0)),
                  pl.BlockSpec((1, d, ff), lambda i, te, nu: (te[i], 0, 0)),
                  pl.BlockSpec((1, d, ff), lambda i, te, nu: (te[i], 0, 0)),
                  pl.BlockSpec((1, ff, d), lambda i, te, nu: (te[i], 0, 0))],
        out_specs=pl.BlockSpec((tm, d), lambda i, te, nu: (i, 0)),
    )
    return pl.pallas_call(
        _moe_ffn_kernel, grid_spec=grid_spec,
        out_shape=jax.ShapeDtypeStruct((r, d), F32),
        compiler_params=_cparams(1), name="moe_ffn",
    )(tile_expert, n_used, xs, gain.reshape(1, d), wg, wu, wd)


def _router_kernel(x_ref, g_ref, wr_ref, idx_ref, prob_ref, *, ne):
    h = _rms(x_ref[...], g_ref[...])
    logits = jnp.dot(h, wr_ref[...], precision=HIGHEST, preferred_element_type=F32)
    col = lax.broadcasted_iota(I32, logits.shape, 1).astype(F32)
    logits = jnp.where(col < ne, logits, -jnp.inf)
    m1 = jnp.max(logits, axis=1, keepdims=True)
    i1 = jnp.min(jnp.where(logits == m1, col, float(ne)), axis=1, keepdims=True)
    rest = jnp.where(col == i1, -jnp.inf, logits)
    m2 = jnp.max(rest, axis=1, keepdims=True)
    i2 = jnp.min(jnp.where(rest == m2, col, float(ne)), axis=1, keepdims=True)
    e2 = jnp.exp(m2 - m1)
    p1 = 1.0 / (1.0 + e2)
    idx_ref[...] = jnp.where(col == 0.0, i1, i2).astype(I32)
    prob_ref[...] = jnp.where(col == 0.0, p1, e2 * p1)


def router_top2(x, gain, wr, tm=ROW_TILE):
    t, d = x.shape
    ne = wr.shape[1]
    wr_pad = jnp.zeros((d, LANE), F32).at[:, :ne].set(wr)
    return pl.pallas_call(
        functools.partial(_router_kernel, ne=ne),
        grid=(t // tm,),
        in_specs=[pl.BlockSpec((tm, d), lambda i: (i, 0)), _const_spec((1, d)), _const_spec((d, LANE))],
        out_specs=[pl.BlockSpec((tm, LANE), lambda i: (i, 0)), pl.BlockSpec((tm, LANE), lambda i: (i, 0))],
        out_shape=[jax.ShapeDtypeStruct((t, LANE), I32), jax.ShapeDtypeStruct((t, LANE), F32)],
        compiler_params=_cparams(1), name="router_top2",
    )(x, gain.reshape(1, d), wr_pad)


def _gather_kernel(idx_ref, x_hbm, o_ref, sem):
    rows = o_ref.shape[0]

    def issue(k, c):
        pltpu.make_async_copy(x_hbm.at[pl.ds(idx_ref[0, 0, k], 1)], o_ref.at[pl.ds(k, 1)], sem).start()
        return c

    lax.fori_loop(0, rows, issue, 0, unroll=8)
    pltpu.make_async_copy(x_hbm.at[pl.ds(0, rows)], o_ref, sem).wait()


def gather_rows(x, src, rows=GATHER_ROWS):
    r = src.shape[0]
    d = x.shape[1]
    return pl.pallas_call(
        _gather_kernel,
        grid=(r // rows,),
        in_specs=[pl.BlockSpec((1, 1, rows), lambda i: (i, 0, 0), memory_space=pltpu.SMEM),
                  pl.BlockSpec(memory_space=pl.ANY)],
        out_specs=pl.BlockSpec((rows, d), lambda i: (i, 0)),
        out_shape=jax.ShapeDtypeStruct((r, d), x.dtype),
        scratch_shapes=[pltpu.SemaphoreType.DMA(())],
        compiler_params=_cparams(1), name="gather_rows",
    )(src.reshape(r // rows, 1, rows), x)


def _combine_kernel(pos_ref, x_ref, p_ref, g_ref, y_hbm, o_ref, buf, sem, *, final_norm):
    tm = x_ref.shape[0]

    def issue(k, c):
        pltpu.make_async_copy(y_hbm.at[pl.ds(pos_ref[0, 0, k], 1)], buf.at[0, pl.ds(k, 1)], sem).start()
        pltpu.make_async_copy(y_hbm.at[pl.ds(pos_ref[0, 1, k], 1)], buf.at[1, pl.ds(k, 1)], sem).start()
        return c

    lax.fori_loop(0, tm, issue, 0, unroll=8)
    pltpu.make_async_copy(y_hbm.at[pl.ds(0, tm)], buf.at[0], sem).wait()
    pltpu.make_async_copy(y_hbm.at[pl.ds(0, tm)], buf.at[1], sem).wait()
    p = p_ref[...]
    out = x_ref[...] + p[:, 0:1] * buf[0] + p[:, 1:2] * buf[1]
    if final_norm:
        out = _rms(out, g_ref[...])
    o_ref[...] = out


def moe_combine(x, ys, pos, prob, final_gain, tm=256):
    t, d = x.shape
    final_norm = final_gain is not None
    gain = final_gain if final_norm else jnp.ones((d,), F32)
    pos3 = pos.reshape(t // tm, tm, 2).transpose(0, 2, 1)
    return pl.pallas_call(
        functools.partial(_combine_kernel, final_norm=final_norm),
        grid=(t // tm,),
        in_specs=[pl.BlockSpec((1, 2, tm), lambda i: (i, 0, 0), memory_space=pltpu.SMEM),
                  pl.BlockSpec((tm, d), lambda i: (i, 0)),
                  pl.BlockSpec((tm, LANE), lambda i: (i, 0)),
                  _const_spec((1, d)),
                  pl.BlockSpec(memory_space=pl.ANY)],
        out_specs=pl.BlockSpec((tm, d), lambda i: (i, 0)),
        out_shape=jax.ShapeDtypeStruct((t, d), F32),
        scratch_shapes=[pltpu.VMEM((2, tm, d), F32), pltpu.SemaphoreType.DMA(())],
        compiler_params=_cparams(1), name="moe_combine",
    )(pos3, x, prob, gain.reshape(1, d), ys)


def _final_norm_kernel(x_ref, g_ref, o_ref):
    o_ref[...] = _rms(x_ref[...], g_ref[...])


def final_norm(x, gain, tm=ROW_TILE):
    t, d = x.shape
    return pl.pallas_call(
        _final_norm_kernel, grid=(t // tm,),
        in_specs=[pl.BlockSpec((tm, d), lambda i: (i, 0)), _const_spec((1, d))],
        out_specs=pl.BlockSpec((tm, d), lambda i: (i, 0)),
        out_shape=jax.ShapeDtypeStruct((t, d), F32),
        compiler_params=_cparams(1), name="final_norm",
    )(x, gain.reshape(1, d))


def _route_plan(idx, tm):
    t = idx.shape[0]
    e_flat = idx.reshape(-1)
    onehot = (e_flat[:, None] == jnp.arange(N_EXPERTS, dtype=I32)[None, :]).astype(I32)
    csum = jnp.cumsum(onehot, axis=0)
    rank = jnp.take_along_axis(csum, e_flat[:, None], axis=1)[:, 0] - 1
    counts = csum[-1]
    padded = ((counts + tm - 1) // tm) * tm
    ends = jnp.cumsum(padded)
    pos = (ends - padded)[e_flat] + rank
    n_rows = 2 * t + N_EXPERTS * tm
    src = jnp.zeros((n_rows,), I32).at[pos].set(jnp.arange(2 * t, dtype=I32) // 2)
    n_used = (ends[-1] // tm).astype(I32)
    tile_start = jnp.arange(n_rows // tm, dtype=I32) * tm
    tile_start = jnp.minimum(tile_start, ends[-1] - 1)
    tile_expert = jnp.sum((ends[None, :] <= tile_start[:, None]).astype(I32), axis=1)
    tile_expert = jnp.minimum(tile_expert, N_EXPERTS - 1)
    return src, pos.reshape(t, 2), tile_expert, n_used.reshape(1)


def moe_layer(x, gain, router, wg, wu, wd, final_gain, tm=MOE_TILE, tc=256):
    idx, prob = router_top2(x, gain, router)
    src, pos, tile_expert, n_used = _route_plan(idx[:, :2], tm)
    xs = gather_rows(x, src, rows=min(GATHER_ROWS, tm))
    ys = moe_ffn(xs, gain, wg, wu, wd, tile_expert, n_used, tm)
    return moe_combine(x, ys, pos, prob, final_gain, tm=tc)


def gla_layer(x, gain, w_in, w_gate_up, b_gate, gnorm, w_out, batch, seq):
    d = x.shape[1]
    n_main = w_in.shape[1] - GLA_GATE_RANK
    w_main = w_in[:, :n_main].astype(BF16)
    w_a = jnp.zeros((d, LANE), F32).at[:, :GLA_GATE_RANK].set(w_in[:, n_main:])
    w_gu = jnp.zeros((LANE, w_gate_up.shape[1]), F32).at[:GLA_GATE_RANK].set(w_gate_up)
    qkvg, la = gla_proj(x, gain, w_main, w_a, w_gu, b_gate)
    og = gla_core(qkvg, la, gnorm, batch, seq, rows=min(GLA_ROWS, seq))
    return proj_residual(x, og, w_out.astype(BF16))


def sba_layer(x, gain, w_in, w_out, batch, seq):
    qkv = norm_proj(x, gain, w_in.astype(BF16))
    o = sba_core(qkv, batch, seq)
    return proj_residual(x, o, w_out.astype(BF16))


def kernel(x, norm_mix, norm_ffn, norm_final, gla_w_in, gla_w_gate_up, gla_b_gate, gla_norm, gla_w_out, sba_w_in, sba_w_out, pool_w_in, pool_w_group, pool_scale, ffn_w_gate, ffn_w_up, ffn_w_down, moe_router, moe_w_gate, moe_w_up, moe_w_down):
    batch, seq, d = x.shape
    depth = norm_mix.shape[0]
    x = x.reshape(batch * seq, d)
    normed = False
    for i in range(depth):
        kind, j = i % 3, i // 3
        if kind == 0:
            x = gla_layer(x, norm_mix[i], gla_w_in[j], gla_w_gate_up[j], gla_b_gate[j],
                          gla_norm[j], gla_w_out[j], batch, seq)
        elif kind == 1:
            x = sba_layer(x, norm_mix[i], sba_w_in[j], sba_w_out[j], batch, seq)
        else:
            x = pool_mixer(x, norm_mix[i], pool_w_in[j].astype(BF16), pool_w_group[j].astype(BF16),
                           pool_scale[j], seq)
        f = i // 2
        if i % 2 == 0:
            x = ffn_dense(x, norm_ffn[i], ffn_w_gate[f].astype(BF16), ffn_w_up[f].astype(BF16),
                          ffn_w_down[f].astype(BF16))
        else:
            last = i == depth - 1
            x = moe_layer(x, norm_ffn[i], moe_router[f], moe_w_gate[f].astype(BF16),
                          moe_w_up[f].astype(BF16), moe_w_down[f].astype(BF16),
                          norm_final if last else None)
            normed = last
    if not normed:
        x = final_norm(x, norm_final)
    return x.reshape(batch, seq, d)
```

```python
import functools

import numpy as np
import jax
import jax.numpy as jnp
from jax import lax
from jax.experimental import pallas as pl
from jax.experimental.pallas import tpu as pltpu

F32 = jnp.float32
BF16 = jnp.bfloat16
I32 = jnp.int32
HIGHEST = lax.Precision.HIGHEST

RMS_EPS = 1e-6
GLA_HEADS = 4
GLA_GATE_RANK = 16
GLA_GATE_NORMALIZER = 16.0
SBA_HEADS = 16
POOL_WINDOWS = (2, 4, 8, 16)
N_EXPERTS = 8

LANE = 128
VMEM_LIMIT = 56 * 1024 * 1024
ROW_TILE = 512
FF_CHUNK = 256
GLA_CHUNK = 64
GLA_SUB = 16
GLA_ROWS = 512
SBA_BQ = 256
SBA_ROWS = 1024
SBA_DEAD2 = 150.0
LOG2E = 1.4426950408889634
MOE_TILE = 512
DISPATCH_ROWS = 256


def _cparams(n_axes):
    return pltpu.CompilerParams(dimension_semantics=("arbitrary",) * n_axes,
                                vmem_limit_bytes=VMEM_LIMIT)


def _rms(xf, gain):
    return xf * lax.rsqrt(jnp.mean(xf * xf, axis=-1, keepdims=True) + RMS_EPS) * gain


def _sigmoid(x):
    return 1.0 / (1.0 + jnp.exp(-x))


def _log_sigmoid(z):
    return jnp.minimum(z, 0.0) - jnp.log(1.0 + jnp.exp(-jnp.abs(z)))


def _dot_split(a, b, split_lhs=False):
    f = a if split_lhs else b
    hi = f.astype(BF16)
    lo = (f - hi.astype(F32)).astype(BF16)
    if split_lhs:
        return (jnp.dot(hi, b, preferred_element_type=F32) + jnp.dot(lo, b, preferred_element_type=F32))
    return (jnp.dot(a, hi, preferred_element_type=F32) + jnp.dot(a, lo, preferred_element_type=F32))


def _const_spec(shape):
    return pl.BlockSpec(shape, lambda *_: (0,) * len(shape))


def _norm_proj_kernel(x_ref, g_ref, w_ref, o_ref, *, nchunk):
    h = _rms(x_ref[...], g_ref[...]).astype(BF16)
    n = o_ref.shape[1]
    for c0 in range(0, n, nchunk):
        o_ref[:, c0:c0 + nchunk] = jnp.dot(
            h, w_ref[:, c0:c0 + nchunk], preferred_element_type=F32).astype(o_ref.dtype)


def norm_proj(x, gain, w, tm=ROW_TILE):
    t, d = x.shape
    n = w.shape[1]
    return pl.pallas_call(
        functools.partial(_norm_proj_kernel, nchunk=512),
        grid=(t // tm,),
        in_specs=[pl.BlockSpec((tm, d), lambda i: (i, 0)),
                  _const_spec((1, d)), _const_spec((d, n))],
        out_specs=pl.BlockSpec((tm, n), lambda i: (i, 0)),
        out_shape=jax.ShapeDtypeStruct((t, n), BF16),
        compiler_params=_cparams(1), name="norm_proj",
    )(x, gain.reshape(1, d), w)


def _gla_proj_kernel(x_ref, g_ref, w_ref, wgu_ref, bg_ref, o_ref, la_ref, *, nchunk):
    h = _rms(x_ref[...], g_ref[...]).astype(BF16)
    n = o_ref.shape[1]
    for c0 in range(0, n, nchunk):
        o_ref[:, c0:c0 + nchunk] = jnp.dot(
            h, w_ref[:, c0:c0 + nchunk], preferred_element_type=F32).astype(o_ref.dtype)
    a = jnp.dot(h, w_ref[:, n:], preferred_element_type=F32)
    z = jnp.dot(a, wgu_ref[...], precision=HIGHEST, preferred_element_type=F32) + bg_ref[...]
    la_ref[...] = _log_sigmoid(z) / GLA_GATE_NORMALIZER


def gla_proj(x, gain, w_all, w_gu, b_gate, tm=ROW_TILE):
    t, d = x.shape
    n = w_all.shape[1] - LANE
    dk = w_gu.shape[1]
    return pl.pallas_call(
        functools.partial(_gla_proj_kernel, nchunk=512),
        grid=(t // tm,),
        in_specs=[pl.BlockSpec((tm, d), lambda i: (i, 0)),
                  _const_spec((1, d)), _const_spec((d, n + LANE)),
                  _const_spec((LANE, dk)), _const_spec((1, dk))],
        out_specs=[pl.BlockSpec((tm, n), lambda i: (i, 0)),
                   pl.BlockSpec((tm, dk), lambda i: (i, 0))],
        out_shape=[jax.ShapeDtypeStruct((t, n), BF16), jax.ShapeDtypeStruct((t, dk), F32)],
        compiler_params=_cparams(1), name="gla_proj",
    )(x, gain.reshape(1, d), w_all, w_gu, b_gate.reshape(1, dk))


def _proj_residual_kernel(x_ref, a_ref, w_ref, o_ref):
    o_ref[...] = x_ref[...] + jnp.dot(a_ref[...], w_ref[...], preferred_element_type=F32)


def proj_residual(x, a, w, tm=ROW_TILE):
    t, d = x.shape
    k = a.shape[1]
    return pl.pallas_call(
        _proj_residual_kernel,
        grid=(t // tm,),
        in_specs=[pl.BlockSpec((tm, d), lambda i: (i, 0)),
                  pl.BlockSpec((tm, k), lambda i: (i, 0)), _const_spec((k, d))],
        out_specs=pl.BlockSpec((tm, d), lambda i: (i, 0)),
        out_shape=jax.ShapeDtypeStruct((t, d), F32),
        compiler_params=_cparams(1), name="proj_residual",
    )(x, a, w)


def _gla_tables():
    c, s = GLA_CHUNK, GLA_SUB
    i = np.arange(c)[:, None]
    j = np.arange(c)[None, :]
    lo = (i // s) * s
    hi = lo + s
    a1 = (j >= lo) & (j <= i)
    a2 = (j > i) & (j < hi)
    a3 = j < lo
    a4 = j >= hi
    amat = np.concatenate([a1, a2, a3, a4], axis=0).astype(np.float32)
    row = np.arange(s * LANE)[:, None] // LANE
    erep = (row == (np.arange(c)[None, :] % s)).astype(np.float32)
    return jnp.asarray(amat, dtype=BF16), jnp.asarray(erep, dtype=BF16)


def _gla_kernel(q_ref, k_ref, v_ref, g_ref, la_ref, gn_ref, amat_ref, erep_ref,
                og_ref, state_s, kf_s, dq_s, tm_s, *, qscale):
    c, s = GLA_CHUNK, GLA_SUB
    nsub = c // s
    nh = state_s.shape[0]
    dk = q_ref.shape[1] // nh
    dv = v_ref.shape[1] // nh

    @pl.when(pl.program_id(1) == 0)
    def _():
        state_s[...] = jnp.zeros_like(state_s)

    il = lax.broadcasted_iota(I32, (s, dk), 0)
    ri = lax.broadcasted_iota(I32, (c, c), 0) // s
    rj = lax.broadcasted_iota(I32, (c, c), 1) // s
    rowblk = lax.broadcasted_iota(I32, (c, dk), 0) // s
    ones_ck = jnp.ones((c, dk), BF16)

    def head_chunk(h, r0):
        kcols = slice(h * dk, (h + 1) * dk)
        vcols = slice(h * dv, (h + 1) * dv)
        la = la_ref[pl.ds(r0, c), kcols]
        qf = q_ref[pl.ds(r0, c), kcols].astype(F32) * qscale
        kf = k_ref[pl.ds(r0, c), kcols].astype(F32)
        vb = v_ref[pl.ds(r0, c), vcols]
        cum = _dot_split(amat_ref[...], la)
        dq, dkk, cprev, ctail = cum[0:c], cum[c:2 * c], cum[2 * c:3 * c], cum[3 * c:4 * c]
        cnext = cprev + dq + dkk
        qt = qf * jnp.exp(dq)
        kt = kf * jnp.exp(dkk)
        kf_s[h] = kf
        dq_s[h] = dq
        for r in range(nsub):
            qb = qf[r * s:(r + 1) * s]
            dqb = dq[r * s:(r + 1) * s]
            for jl in range(s):
                krow = kf_s[h, pl.ds(r * s + jl, 1), :]
                drow = dq_s[h, pl.ds(r * s + jl, 1), :]
                dec = jnp.where(il >= jl, jnp.exp(jnp.minimum(dqb - drow, 0.0)), 0.0)
                tm_s[h, r * s:(r + 1) * s, jl * dk:(jl + 1) * dk] = (qb * krow * dec).astype(BF16)
        sdiag = jnp.dot(tm_s[h], erep_ref[...], preferred_element_type=F32)
        soff = jnp.zeros((c, c), F32)
        for rp in range(nsub - 1):
            cn = cnext[rp * s:rp * s + 1, :]
            p = (qt * jnp.exp(jnp.minimum(cprev - cn, 0.0))).astype(BF16)
            kp = jnp.where(rowblk == rp, kt, 0.0).astype(BF16)
            soff = soff + lax.dot_general(p, kp, (((1,), (1,)), ((), ())),
                                          preferred_element_type=F32)
        sc = jnp.where(ri == rj, sdiag, jnp.where(ri > rj, soff, 0.0))
        st = state_s[h]
        o = jnp.dot(sc.astype(BF16), vb, preferred_element_type=F32) + jnp.dot(
            (qt * jnp.exp(cprev)).astype(BF16), st.astype(BF16), preferred_element_type=F32)
        ks_t = (kt * jnp.exp(ctail)).T.astype(BF16)
        upd = jnp.dot(ks_t, vb, preferred_element_type=F32)
        dcol = jnp.exp(_dot_split(la.T, ones_ck, split_lhs=True))
        state_s[h] = jnp.concatenate([dcol] * (dv // dk), axis=1) * st + upd
        gf = g_ref[pl.ds(r0, c), vcols].astype(F32)
        og_ref[pl.ds(r0, c), vcols] = (_rms(o, gn_ref[...]) * (gf * _sigmoid(gf))).astype(og_ref.dtype)

    def chunk(ci, carry):
        r0 = pl.multiple_of(ci * c, c)
        for h in range(nh):
            head_chunk(h, r0)
        return carry

    lax.fori_loop(0, q_ref.shape[0] // c, chunk, 0)


def gla_core(qkvg, la, gnorm, batch, seq, rows=GLA_ROWS):
    t = qkvg.shape[0]
    dkt = la.shape[1]
    dk = dkt // GLA_HEADS
    dvt = (qkvg.shape[1] - 2 * dkt) // 2
    dv = dvt // GLA_HEADS
    nblk = seq // rows
    amat, erep = _gla_tables()
    row = lambda b, c: b * nblk + c
    return pl.pallas_call(
        functools.partial(_gla_kernel, qscale=float(dk) ** -0.5),
        grid=(batch, nblk),
        in_specs=[pl.BlockSpec((rows, dkt), lambda b, c: (row(b, c), 0)),
                  pl.BlockSpec((rows, dkt), lambda b, c: (row(b, c), 1)),
                  pl.BlockSpec((rows, dvt), lambda b, c: (row(b, c), 2 * dkt // dvt)),
                  pl.BlockSpec((rows, dvt), lambda b, c: (row(b, c), 2 * dkt // dvt + 1)),
                  pl.BlockSpec((rows, dkt), lambda b, c: (row(b, c), 0)),
                  _const_spec((1, dv)), _const_spec(amat.shape), _const_spec(erep.shape)],
        out_specs=pl.BlockSpec((rows, dvt), lambda b, c: (row(b, c), 0)),
        out_shape=jax.ShapeDtypeStruct((t, dvt), BF16),
        scratch_shapes=[pltpu.VMEM((GLA_HEADS, dk, dv), F32),
                        pltpu.VMEM((GLA_HEADS, GLA_CHUNK, dk), F32),
                        pltpu.VMEM((GLA_HEADS, GLA_CHUNK, dk), F32),
                        pltpu.VMEM((GLA_HEADS, GLA_CHUNK, GLA_SUB * dk), BF16)],
        compiler_params=_cparams(2), name="gla_core",
    )(qkvg, qkvg, qkvg, qkvg, la, gnorm.reshape(1, dv), amat, erep)


def _sba_kernel(q_ref, k_ref, v_ref, u_ref, o_ref, qs_s, acc_s, carry_s, *, scale, dh):
    bq = SBA_BQ
    nq = q_ref.shape[0] // bq
    step = pl.program_id(2)
    lane = lax.broadcasted_iota(I32, (bq, LANE), 1)
    row = lax.broadcasted_iota(I32, (2 * bq, bq), 0)
    strict = lax.broadcasted_iota(I32, (2 * bq, bq), 1) < jnp.where(row >= bq, row - bq, row)

    def block(jb, diagonal):
        k0 = pl.multiple_of(jb * bq, bq)
        kb = k_ref[pl.ds(k0, bq), :]
        vb = v_ref[pl.ds(k0, bq), :]
        z = lax.dot_general(qs_s[...], kb, (((1,), (1,)), ((), ())), preferred_element_type=F32)
        sp = jnp.maximum(z, 0.0) + jnp.log2(1.0 + jnp.exp2(jnp.minimum(z, -z)))
        if diagonal:
            sp = jnp.where(strict, sp, 0.0)
        ct = jnp.dot(sp.astype(BF16), u_ref[...], preferred_element_type=F32)
        carry = carry_s[...]
        w = jnp.exp2((z - sp) - ct[:, :bq] - jnp.concatenate([carry] * (bq // LANE), axis=1))
        if diagonal:
            w = jnp.where(strict, w, 0.0)
        acc_s[...] += jnp.dot(w.astype(BF16), vb, preferred_element_type=F32)
        carry = carry + ct[:, bq:]
        carry_s[...] = carry
        return jnp.min(carry)

    def qblock(qi, c):
        r0 = pl.multiple_of(qi * bq, bq)
        q2 = q_ref[pl.ds(r0, bq), :].astype(F32) * scale
        qs_s[0:bq] = jnp.where(lane < dh, q2, 0.0).astype(BF16)
        qs_s[bq:2 * bq] = jnp.where(lane >= dh, q2, 0.0).astype(BF16)
        acc_s[...] = jnp.zeros_like(acc_s)
        carry_s[...] = jnp.zeros_like(carry_s)
        jd = step * nq + qi
        low = block(jd, True)
        lax.while_loop(lambda st: jnp.logical_and(st[0] >= 0, st[1] < SBA_DEAD2),
                       lambda st: (st[0] - 1, block(st[0], False)), (jd - 1, low))
        a = acc_s[...]
        o_ref[pl.ds(r0, bq), :] = jnp.where(lane < dh, a[:bq], a[bq:]).astype(o_ref.dtype)
        return c

    lax.fori_loop(0, nq, qblock, 0)


def sba_core(qkv, batch, seq):
    t = qkv.shape[0]
    d = qkv.shape[1] // 3
    dh = d // SBA_HEADS
    npair = d // LANE
    bq = SBA_BQ
    rows = min(SBA_ROWS, seq)
    nstep = seq // rows
    u = np.concatenate([np.tril(np.ones((bq, bq), np.float32), -1), np.ones((bq, LANE), np.float32)], 1)
    return pl.pallas_call(
        functools.partial(_sba_kernel, scale=float(dh) ** -0.5 * LOG2E, dh=dh),
        grid=(batch, npair, nstep),
        in_specs=[pl.BlockSpec((rows, LANE), lambda b, p, i: (b * nstep + i, p)),
                  pl.BlockSpec((seq, LANE), lambda b, p, i: (b, npair + p)),
                  pl.BlockSpec((seq, LANE), lambda b, p, i: (b, 2 * npair + p)),
                  _const_spec((bq, bq + LANE))],
        out_specs=pl.BlockSpec((rows, LANE), lambda b, p, i: (b * nstep + i, p)),
        out_shape=jax.ShapeDtypeStruct((t, d), BF16),
        scratch_shapes=[pltpu.VMEM((2 * bq, LANE), BF16), pltpu.VMEM((2 * bq, LANE), F32),
                        pltpu.VMEM((2 * bq, LANE), F32)],
        compiler_params=_cparams(3), name="sba_core",
    )(qkv, qkv, qkv, jnp.asarray(u, dtype=BF16))


def _pool_kernel(x_ref, halo_ref, g_ref, win_ref, wgrp_ref, sc_ref, o_ref, u_s, *, tiles_per_seq):
    tm, d = x_ref.shape
    hal = halo_ref.shape[0]
    gd = wgrp_ref.shape[1]
    first = (pl.program_id(0) % tiles_per_seq) == 0
    x = x_ref[...]
    xx = jnp.concatenate([halo_ref[...], x], axis=0)
    h = _rms(xx, g_ref[...]).astype(BF16)
    u = jnp.dot(h, win_ref[...], preferred_element_type=F32)
    keep = jnp.where(first, 0.0, 1.0)
    rowi = lax.broadcasted_iota(I32, (hal + tm, 1), 0)
    u_s[...] = u * jnp.where(rowi < hal, keep, 1.0)
    pos = (pl.program_id(0) % tiles_per_seq) * tm + lax.broadcasted_iota(I32, (tm, 1), 0)
    mixed = []
    for gi, w in enumerate(POOL_WINDOWS):
        cs = slice(gi * gd, (gi + 1) * gd)
        ssum = u_s[hal:hal + tm, cs]
        for dlt in range(1, w):
            ssum = ssum + u_s[hal - dlt:hal - dlt + tm, cs]
        cnt = jnp.minimum(pos + 1, w).astype(F32)
        pooled = ssum / cnt - u_s[hal:hal + tm, cs]
        mixed.append(jnp.dot(pooled.astype(BF16), wgrp_ref[gi], preferred_element_type=F32))
    o_ref[...] = x + jnp.concatenate(mixed, axis=1) * sc_ref[...]


def pool_mixer(x, gain, w_in, w_group, scale, seq, tm=ROW_TILE):
    t, d = x.shape
    hal = max(POOL_WINDOWS)
    g, gd, _ = w_group.shape
    return pl.pallas_call(
        functools.partial(_pool_kernel, tiles_per_seq=seq // tm),
        grid=(t // tm,),
        in_specs=[pl.BlockSpec((tm, d), lambda i: (i, 0)),
                  pl.BlockSpec((hal, d), lambda i: (jnp.maximum(i * (tm // hal) - 1, 0), 0)),
                  _const_spec((1, d)), _const_spec((d, d)), _const_spec((g, gd, gd)),
                  _const_spec((1, d))],
        out_specs=pl.BlockSpec((tm, d), lambda i: (i, 0)),
        out_shape=jax.ShapeDtypeStruct((t, d), F32),
        scratch_shapes=[pltpu.VMEM((hal + tm, d), F32)],
        compiler_params=_cparams(1), name="pool_mixer",
    )(x, x, gain.reshape(1, d), w_in, w_group, scale.reshape(1, d))


def _swiglu_acc(h, wg_ref, wu_ref, wd_ref, acc, widx):
    ff = wg_ref.shape[-1]
    for c0 in range(0, ff, FF_CHUNK):
        cs = slice(c0, c0 + FF_CHUNK)
        gate = jnp.dot(h, wg_ref[widx + (slice(None), cs)], preferred_element_type=F32)
        up = jnp.dot(h, wu_ref[widx + (slice(None), cs)], preferred_element_type=F32)
        a = (gate * _sigmoid(gate) * up).astype(BF16)
        acc = acc + jnp.dot(a, wd_ref[widx + (cs, slice(None))], preferred_element_type=F32)
    return acc


def _ffn_kernel(x_ref, g_ref, wg_ref, wu_ref, wd_ref, o_ref):
    x = x_ref[...]
    h = _rms(x, g_ref[...]).astype(BF16)
    o_ref[...] = _swiglu_acc(h, wg_ref, wu_ref, wd_ref, x, ())


def ffn_dense(x, gain, wg, wu, wd, tm=ROW_TILE):
    t, d = x.shape
    ff = wg.shape[1]
    return pl.pallas_call(
        _ffn_kernel,
        grid=(t // tm,),
        in_specs=[pl.BlockSpec((tm, d), lambda i: (i, 0)), _const_spec((1, d)),
                  _const_spec((d, ff)), _const_spec((d, ff)), _const_spec((ff, d))],
        out_specs=pl.BlockSpec((tm, d), lambda i: (i, 0)),
        out_shape=jax.ShapeDtypeStruct((t, d), F32),
        compiler_params=_cparams(1), name="ffn_dense",
    )(x, gain.reshape(1, d), wg, wu, wd)


def _moe_ffn_kernel(texp_ref, nused_ref, x_ref, g_ref, wg_ref, wu_ref, wd_ref, o_ref):
    live = pl.program_id(0) < nused_ref[0]

    @pl.when(live)
    def _():
        h = _rms(x_ref[...], g_ref[...]).astype(BF16)
        o_ref[...] = _swiglu_acc(h, wg_ref, wu_ref, wd_ref, jnp.zeros(o_ref.shape, F32), (0,))

    @pl.when(jnp.logical_not(live))
    def _():
        o_ref[...] = jnp.zeros_like(o_ref)


def moe_ffn(xs, gain, wg, wu, wd, tile_expert, n_used, tm):
    r, d = xs.shape
    ff = wg.shape[2]
    live = lambda i, te, nu: jnp.minimum(i, nu[0] - 1)
    grid_spec = pltpu.PrefetchScalarGridSpec(
        num_scalar_prefetch=2,
        grid=(r // tm,),
        in_specs=[pl.BlockSpec((tm, d), lambda i, te, nu: (live(i, te, nu), 0)),
                  pl.BlockSpec((1, d), lambda i, te, nu: (0, 0)),
                  pl.BlockSpec((1, d, ff), lambda i, te, nu: (te[i], 0, 0)),
                  pl.BlockSpec((1, d, ff), lambda i, te, nu: (te[i], 0, 0)),
                  pl.BlockSpec((1, ff, d), lambda i, te, nu: (te[i], 0, 0))],
        out_specs=pl.BlockSpec((tm, d), lambda i, te, nu: (i, 0)),
    )
    return pl.pallas_call(
        _moe_ffn_kernel, grid_spec=grid_spec,
        out_shape=jax.ShapeDtypeStruct((r, d), F32),
        compiler_params=_cparams(1), name="moe_ffn",
    )(tile_expert, n_used, xs, gain.reshape(1, d), wg, wu, wd)


def _router_kernel(x_ref, g_ref, wr_ref, idx_ref, prob_ref, *, ne):
    h = _rms(x_ref[...], g_ref[...])
    logits = jnp.dot(h, wr_ref[...], precision=HIGHEST, preferred_element_type=F32)
    col = lax.broadcasted_iota(I32, logits.shape, 1).astype(F32)
    logits = jnp.where(col < ne, logits, -jnp.inf)
    m1 = jnp.max(logits, axis=1, keepdims=True)
    i1 = jnp.min(jnp.where(logits == m1, col, float(ne)), axis=1, keepdims=True)
    rest = jnp.where(col == i1, -jnp.inf, logits)
    m2 = jnp.max(rest, axis=1, keepdims=True)
    i2 = jnp.min(jnp.where(rest == m2, col, float(ne)), axis=1, keepdims=True)
    e2 = jnp.exp(m2 - m1)
    p1 = 1.0 / (1.0 + e2)
    idx_ref[...] = jnp.where(col == 0.0, i1, i2).astype(I32)
    prob_ref[...] = jnp.where(col == 0.0, p1, e2 * p1)


def router_top2(x, gain, wr, tm=ROW_TILE):
    t, d = x.shape
    ne = wr.shape[1]
    wr_pad = jnp.zeros((d, LANE), F32).at[:, :ne].set(wr)
    return pl.pallas_call(
        functools.partial(_router_kernel, ne=ne),
        grid=(t // tm,),
        in_specs=[pl.BlockSpec((tm, d), lambda i: (i, 0)), _const_spec((1, d)), _const_spec((d, LANE))],
        out_specs=[pl.BlockSpec((tm, LANE), lambda i: (i, 0)), pl.BlockSpec((tm, LANE), lambda i: (i, 0))],
        out_shape=[jax.ShapeDtypeStruct((t, LANE), I32), jax.ShapeDtypeStruct((t, LANE), F32)],
        compiler_params=_cparams(1), name="router_top2",
    )(x, gain.reshape(1, d), wr_pad)


def _dispatch_kernel(pos_ref, x_ref, xs_in, xs_hbm, sem):
    del xs_in
    tm = x_ref.shape[0]

    def issue(k, c):
        pltpu.make_async_copy(x_ref.at[pl.ds(k, 1)], xs_hbm.at[pl.ds(pos_ref[0, 0, k], 1)], sem).start()
        pltpu.make_async_copy(x_ref.at[pl.ds(k, 1)], xs_hbm.at[pl.ds(pos_ref[0, 1, k], 1)], sem).start()
        return c

    lax.fori_loop(0, tm, issue, 0, unroll=8)
    pltpu.make_async_copy(x_ref, xs_hbm.at[pl.ds(0, tm)], sem).wait()
    pltpu.make_async_copy(x_ref, xs_hbm.at[pl.ds(0, tm)], sem).wait()


def dispatch_rows(x, pos, n_rows, tm=DISPATCH_ROWS):
    t, d = x.shape
    pos3 = pos.reshape(t // tm, tm, 2).transpose(0, 2, 1)
    return pl.pallas_call(
        _dispatch_kernel,
        grid=(t // tm,),
        in_specs=[pl.BlockSpec((1, 2, tm), lambda i: (i, 0, 0), memory_space=pltpu.SMEM),
                  pl.BlockSpec((tm, d), lambda i: (i, 0)),
                  pl.BlockSpec(memory_space=pl.ANY)],
        out_specs=pl.BlockSpec(memory_space=pl.ANY),
        out_shape=jax.ShapeDtypeStruct((n_rows, d), x.dtype),
        input_output_aliases={2: 0},
        scratch_shapes=[pltpu.SemaphoreType.DMA(())],
        compiler_params=_cparams(1), name="dispatch_rows",
    )(pos3, x, jnp.zeros((n_rows, d), x.dtype))


def _combine_kernel(pos_ref, x_ref, p_ref, g_ref, y_hbm, o_ref, buf, sem, *, final_norm):
    tm = x_ref.shape[0]

    def issue(k, c):
        pltpu.make_async_copy(y_hbm.at[pl.ds(pos_ref[0, 0, k], 1)], buf.at[0, pl.ds(k, 1)], sem).start()
        pltpu.make_async_copy(y_hbm.at[pl.ds(pos_ref[0, 1, k], 1)], buf.at[1, pl.ds(k, 1)], sem).start()
        return c

    lax.fori_loop(0, tm, issue, 0, unroll=8)
    pltpu.make_async_copy(y_hbm.at[pl.ds(0, tm)], buf.at[0], sem).wait()
    pltpu.make_async_copy(y_hbm.at[pl.ds(0, tm)], buf.at[1], sem).wait()
    p = p_ref[...]
    out = x_ref[...] + p[:, 0:1] * buf[0] + p[:, 1:2] * buf[1]
    if final_norm:
        out = _rms(out, g_ref[...])
    o_ref[...] = out


def moe_combine(x, ys, pos, prob, final_gain, tm=256):
    t, d = x.shape
    final_norm = final_gain is not None
    gain = final_gain if final_norm else jnp.ones((d,), F32)
    pos3 = pos.reshape(t // tm, tm, 2).transpose(0, 2, 1)
    return pl.pallas_call(
        functools.partial(_combine_kernel, final_norm=final_norm),
        grid=(t // tm,),
        in_specs=[pl.BlockSpec((1, 2, tm), lambda i: (i, 0, 0), memory_space=pltpu.SMEM),
                  pl.BlockSpec((tm, d), lambda i: (i, 0)),
                  pl.BlockSpec((tm, LANE), lambda i: (i, 0)),
                  _const_spec((1, d)),
                  pl.BlockSpec(memory_space=pl.ANY)],
        out_specs=pl.BlockSpec((tm, d), lambda i: (i, 0)),
        out_shape=jax.ShapeDtypeStruct((t, d), F32),
        scratch_shapes=[pltpu.VMEM((2, tm, d), F32), pltpu.SemaphoreType.DMA(())],
        compiler_params=_cparams(1), name="moe_combine",
    )(pos3, x, prob, gain.reshape(1, d), ys)


def _final_norm_kernel(x_ref, g_ref, o_ref):
    o_ref[...] = _rms(x_ref[...], g_ref[...])


def final_norm(x, gain, tm=ROW_TILE):
    t, d = x.shape
    return pl.pallas_call(
        _final_norm_kernel, grid=(t // tm,),
        in_specs=[pl.BlockSpec((tm, d), lambda i: (i, 0)), _const_spec((1, d))],
        out_specs=pl.BlockSpec((tm, d), lambda i: (i, 0)),
        out_shape=jax.ShapeDtypeStruct((t, d), F32),
        compiler_params=_cparams(1), name="final_norm",
    )(x, gain.reshape(1, d))


def _route_plan(idx, tm):
    t = idx.shape[0]
    e_flat = idx.reshape(-1)
    onehot = (e_flat[:, None] == jnp.arange(N_EXPERTS, dtype=I32)[None, :]).astype(I32)
    csum = jnp.cumsum(onehot, axis=0)
    rank = jnp.take_along_axis(csum, e_flat[:, None], axis=1)[:, 0] - 1
    counts = csum[-1]
    padded = ((counts + tm - 1) // tm) * tm
    ends = jnp.cumsum(padded)
    pos = (ends - padded)[e_flat] + rank
    n_rows = 2 * t + N_EXPERTS * tm
    n_used = (ends[-1] // tm).astype(I32)
    tile_start = jnp.arange(n_rows // tm, dtype=I32) * tm
    tile_start = jnp.minimum(tile_start, ends[-1] - 1)
    tile_expert = jnp.sum((ends[None, :] <= tile_start[:, None]).astype(I32), axis=1)
    tile_expert = jnp.minimum(tile_expert, N_EXPERTS - 1)
    return pos.reshape(t, 2), tile_expert, n_used.reshape(1), n_rows


def moe_layer(x, gain, router, wg, wu, wd, final_gain, tm=MOE_TILE, tc=256):
    idx, prob = router_top2(x, gain, router)
    pos, tile_expert, n_used, n_rows = _route_plan(idx[:, :2], tm)
    xs = dispatch_rows(x, pos, n_rows, tm=tc)
    ys = moe_ffn(xs, gain, wg, wu, wd, tile_expert, n_used, tm)
    return moe_combine(x, ys, pos, prob, final_gain, tm=tc)


def gla_layer(x, gain, w_in, w_gate_up, b_gate, gnorm, w_out, batch, seq):
    d = x.shape[1]
    n_main = w_in.shape[1] - GLA_GATE_RANK
    w_all = jnp.zeros((d, n_main + LANE), BF16).at[:, :w_in.shape[1]].set(w_in.astype(BF16))
    w_gu = jnp.zeros((LANE, w_gate_up.shape[1]), F32).at[:GLA_GATE_RANK].set(w_gate_up)
    qkvg, la = gla_proj(x, gain, w_all, w_gu, b_gate)
    og = gla_core(qkvg, la, gnorm, batch, seq, rows=min(GLA_ROWS, seq))
    return proj_residual(x, og, w_out.astype(BF16))


def sba_layer(x, gain, w_in, w_out, batch, seq):
    qkv = norm_proj(x, gain, w_in.astype(BF16))
    o = sba_core(qkv, batch, seq)
    return proj_residual(x, o, w_out.astype(BF16))


def kernel(x, norm_mix, norm_ffn, norm_final, gla_w_in, gla_w_gate_up, gla_b_gate, gla_norm, gla_w_out, sba_w_in, sba_w_out, pool_w_in, pool_w_group, pool_scale, ffn_w_gate, ffn_w_up, ffn_w_down, moe_router, moe_w_gate, moe_w_up, moe_w_down):
    batch, seq, d = x.shape
    depth = norm_mix.shape[0]
    x = x.reshape(batch * seq, d)
    normed = False
    for i in range(depth):
        kind, j = i % 3, i // 3
        if kind == 0:
            x = gla_layer(x, norm_mix[i], gla_w_in[j], gla_w_gate_up[j], gla_b_gate[j],
                          gla_norm[j], gla_w_out[j], batch, seq)
        elif kind == 1:
            x = sba_layer(x, norm_mix[i], sba_w_in[j], sba_w_out[j], batch, seq)
        else:
            x = pool_mixer(x, norm_mix[i], pool_w_in[j].astype(BF16), pool_w_group[j].astype(BF16),
                           pool_scale[j], seq)
        f = i // 2
        if i % 2 == 0:
            x = ffn_dense(x, norm_ffn[i], ffn_w_gate[f].astype(BF16), ffn_w_up[f].astype(BF16),
                          ffn_w_down[f].astype(BF16))
        else:
            last = i == depth - 1
            x = moe_layer(x, norm_ffn[i], moe_router[f], moe_w_gate[f].astype(BF16),
                          moe_w_up[f].astype(BF16), moe_w_down[f].astype(BF16),
                          norm_final if last else None)
            normed = last
    if not normed:
        x = final_norm(x, norm_final)
    return x.reshape(batch, seq, d)
```

```python
import functools

import numpy as np
import jax
import jax.numpy as jnp
from jax import lax
from jax.experimental import pallas as pl
from jax.experimental.pallas import tpu as pltpu

F32 = jnp.float32
BF16 = jnp.bfloat16
I32 = jnp.int32
HIGHEST = lax.Precision.HIGHEST

RMS_EPS = 1e-6
GLA_HEADS = 4
GLA_GATE_RANK = 16
GLA_GATE_NORMALIZER = 16.0
SBA_HEADS = 16
POOL_WINDOWS = (2, 4, 8, 16)
N_EXPERTS = 8

LANE = 128
VMEM_LIMIT = 56 * 1024 * 1024
ROW_TILE = 512
FF_CHUNK = 256
GLA_CHUNK = 64
GLA_SUB = 16
GLA_ROWS = 512
GLA_SAFE_EXP = 60.0
SBA_BQ = 256
SBA_ROWS = 1024
SBA_DEAD2 = 150.0
LOG2E = 1.4426950408889634
MOE_TILE = 512
DISPATCH_ROWS = 256


def _cparams(n_axes):
    return pltpu.CompilerParams(dimension_semantics=("arbitrary",) * n_axes,
                                vmem_limit_bytes=VMEM_LIMIT)


def _rms(xf, gain):
    return xf * lax.rsqrt(jnp.mean(xf * xf, axis=-1, keepdims=True) + RMS_EPS) * gain


def _sigmoid(x):
    return 1.0 / (1.0 + jnp.exp(-x))


def _log_sigmoid(z):
    return jnp.minimum(z, 0.0) - jnp.log(1.0 + jnp.exp(-jnp.abs(z)))


def _dot_split(a, b, split_lhs=False):
    f = a if split_lhs else b
    hi = f.astype(BF16)
    lo = (f - hi.astype(F32)).astype(BF16)
    if split_lhs:
        return (jnp.dot(hi, b, preferred_element_type=F32) + jnp.dot(lo, b, preferred_element_type=F32))
    return (jnp.dot(a, hi, preferred_element_type=F32) + jnp.dot(a, lo, preferred_element_type=F32))


def _const_spec(shape):
    return pl.BlockSpec(shape, lambda *_: (0,) * len(shape))


def _norm_proj_kernel(x_ref, g_ref, w_ref, o_ref, *, nchunk):
    h = _rms(x_ref[...], g_ref[...]).astype(BF16)
    n = o_ref.shape[1]
    for c0 in range(0, n, nchunk):
        o_ref[:, c0:c0 + nchunk] = jnp.dot(
            h, w_ref[:, c0:c0 + nchunk], preferred_element_type=F32).astype(o_ref.dtype)


def norm_proj(x, gain, w, tm=ROW_TILE):
    t, d = x.shape
    n = w.shape[1]
    return pl.pallas_call(
        functools.partial(_norm_proj_kernel, nchunk=512),
        grid=(t // tm,),
        in_specs=[pl.BlockSpec((tm, d), lambda i: (i, 0)),
                  _const_spec((1, d)), _const_spec((d, n))],
        out_specs=pl.BlockSpec((tm, n), lambda i: (i, 0)),
        out_shape=jax.ShapeDtypeStruct((t, n), BF16),
        compiler_params=_cparams(1), name="norm_proj",
    )(x, gain.reshape(1, d), w)


def _gla_proj_kernel(x_ref, g_ref, w_ref, wgu_ref, bg_ref, o_ref, la_ref, *, nchunk):
    h = _rms(x_ref[...], g_ref[...]).astype(BF16)
    n = o_ref.shape[1]
    for c0 in range(0, n, nchunk):
        o_ref[:, c0:c0 + nchunk] = jnp.dot(
            h, w_ref[:, c0:c0 + nchunk], preferred_element_type=F32).astype(o_ref.dtype)
    a = jnp.dot(h, w_ref[:, n:], preferred_element_type=F32)
    z = jnp.dot(a, wgu_ref[...], precision=HIGHEST, preferred_element_type=F32) + bg_ref[...]
    la_ref[...] = _log_sigmoid(z) / GLA_GATE_NORMALIZER


def gla_proj(x, gain, w_all, w_gu, b_gate, tm=ROW_TILE):
    t, d = x.shape
    n = w_all.shape[1] - LANE
    dk = w_gu.shape[1]
    return pl.pallas_call(
        functools.partial(_gla_proj_kernel, nchunk=512),
        grid=(t // tm,),
        in_specs=[pl.BlockSpec((tm, d), lambda i: (i, 0)),
                  _const_spec((1, d)), _const_spec((d, n + LANE)),
                  _const_spec((LANE, dk)), _const_spec((1, dk))],
        out_specs=[pl.BlockSpec((tm, n), lambda i: (i, 0)),
                   pl.BlockSpec((tm, dk), lambda i: (i, 0))],
        out_shape=[jax.ShapeDtypeStruct((t, n), BF16), jax.ShapeDtypeStruct((t, dk), F32)],
        compiler_params=_cparams(1), name="gla_proj",
    )(x, gain.reshape(1, d), w_all, w_gu, b_gate.reshape(1, dk))


def _gla_tables():
    c, s = GLA_CHUNK, GLA_SUB
    i = np.arange(c)[:, None]
    j = np.arange(c)[None, :]
    lo = (i // s) * s
    hi = lo + s
    a1 = (j >= lo) & (j <= i)
    a2 = (j > i) & (j < hi)
    a3 = j < lo
    a4 = j >= hi
    amat = np.concatenate([a1, a2, a3, a4], axis=0).astype(np.float32)
    row = np.arange(s * LANE)[:, None] // LANE
    erep = (row == (np.arange(c)[None, :] % s)).astype(np.float32)
    return jnp.asarray(amat, dtype=BF16), jnp.asarray(erep, dtype=BF16)


def _gla_kernel(q_ref, k_ref, v_ref, g_ref, la_ref, gn_ref, amat_ref, erep_ref,
                og_ref, state_s, kf_s, dq_s, tm_s, *, qscale):
    c, s = GLA_CHUNK, GLA_SUB
    nsub = c // s
    nh = state_s.shape[0]
    dk = q_ref.shape[1] // nh
    dv = v_ref.shape[1] // nh

    @pl.when(pl.program_id(1) == 0)
    def _():
        state_s[...] = jnp.zeros_like(state_s)

    il = lax.broadcasted_iota(I32, (s, dk), 0)
    ii = lax.broadcasted_iota(I32, (c, c), 0)
    jj = lax.broadcasted_iota(I32, (c, c), 1)
    ri, rj = ii // s, jj // s
    on_diag = jnp.logical_and(ri == rj, ii >= jj)
    rowblk = lax.broadcasted_iota(I32, (c, dk), 0) // s
    ones_ck = jnp.ones((c, dk), BF16)

    def head_chunk(h, r0, exact_diag):
        kcols = slice(h * dk, (h + 1) * dk)
        vcols = slice(h * dv, (h + 1) * dv)
        la = la_ref[pl.ds(r0, c), kcols]
        qf = q_ref[pl.ds(r0, c), kcols].astype(F32) * qscale
        kf = k_ref[pl.ds(r0, c), kcols].astype(F32)
        vb = v_ref[pl.ds(r0, c), vcols]
        cum = _dot_split(amat_ref[...], la)
        dq, dkk, cprev, ctail = cum[0:c], cum[c:2 * c], cum[2 * c:3 * c], cum[3 * c:4 * c]
        cnext = cprev + dq + dkk
        qt = qf * jnp.exp(dq)
        kt = kf * jnp.exp(dkk)
        if exact_diag:
            kf_s[h] = kf
            dq_s[h] = dq
            for r in range(nsub):
                qb = qf[r * s:(r + 1) * s]
                dqb = dq[r * s:(r + 1) * s]
                for jl in range(s):
                    krow = kf_s[h, pl.ds(r * s + jl, 1), :]
                    drow = dq_s[h, pl.ds(r * s + jl, 1), :]
                    dec = jnp.where(il >= jl, jnp.exp(jnp.minimum(dqb - drow, 0.0)), 0.0)
                    tm_s[h, r * s:(r + 1) * s, jl * dk:(jl + 1) * dk] = (qb * krow * dec).astype(BF16)
            sdiag = jnp.dot(tm_s[h], erep_ref[...], preferred_element_type=F32)
        else:
            kd = (kf * jnp.exp(-dq)).astype(BF16)
            sdiag = lax.dot_general(qt.astype(BF16), kd, (((1,), (1,)), ((), ())),
                                    preferred_element_type=F32)
        soff = jnp.zeros((c, c), F32)
        for rp in range(nsub - 1):
            cn = cnext[rp * s:rp * s + 1, :]
            p = (qt * jnp.exp(jnp.minimum(cprev - cn, 0.0))).astype(BF16)
            kp = jnp.where(rowblk == rp, kt, 0.0).astype(BF16)
            soff = soff + lax.dot_general(p, kp, (((1,), (1,)), ((), ())),
                                          preferred_element_type=F32)
        sc = jnp.where(on_diag, sdiag, jnp.where(ri > rj, soff, 0.0))
        st = state_s[h]
        o = jnp.dot(sc.astype(BF16), vb, preferred_element_type=F32) + jnp.dot(
            (qt * jnp.exp(cprev)).astype(BF16), st.astype(BF16), preferred_element_type=F32)
        ks_t = (kt * jnp.exp(ctail)).T.astype(BF16)
        upd = jnp.dot(ks_t, vb, preferred_element_type=F32)
        dcol = jnp.exp(_dot_split(la.T, ones_ck, split_lhs=True))
        state_s[h] = jnp.concatenate([dcol] * (dv // dk), axis=1) * st + upd
        gf = g_ref[pl.ds(r0, c), vcols].astype(F32)
        og_ref[pl.ds(r0, c), vcols] = (_rms(o, gn_ref[...]) * (gf * _sigmoid(gf))).astype(og_ref.dtype)

    def sweep(exact_diag):
        def chunk(ci, carry):
            r0 = pl.multiple_of(ci * c, c)
            for h in range(nh):
                head_chunk(h, r0, exact_diag)
            return carry

        lax.fori_loop(0, q_ref.shape[0] // c, chunk, 0, unroll=1 if exact_diag else 2)

    safe = jnp.max(-la_ref[...]) * s < GLA_SAFE_EXP
    pl.when(safe)(lambda: sweep(False))
    pl.when(jnp.logical_not(safe))(lambda: sweep(True))


def gla_core(qkvg, la, gnorm, batch, seq, rows=GLA_ROWS):
    t = qkvg.shape[0]
    dkt = la.shape[1]
    dk = dkt // GLA_HEADS
    dvt = (qkvg.shape[1] - 2 * dkt) // 2
    dv = dvt // GLA_HEADS
    nblk = seq // rows
    amat, erep = _gla_tables()
    row = lambda b, c: b * nblk + c
    return pl.pallas_call(
        functools.partial(_gla_kernel, qscale=float(dk) ** -0.5),
        grid=(batch, nblk),
        in_specs=[pl.BlockSpec((rows, dkt), lambda b, c: (row(b, c), 0)),
                  pl.BlockSpec((rows, dkt), lambda b, c: (row(b, c), 1)),
                  pl.BlockSpec((rows, dvt), lambda b, c: (row(b, c), 2 * dkt // dvt)),
                  pl.BlockSpec((rows, dvt), lambda b, c: (row(b, c), 2 * dkt // dvt + 1)),
                  pl.BlockSpec((rows, dkt), lambda b, c: (row(b, c), 0)),
                  _const_spec((1, dv)), _const_spec(amat.shape), _const_spec(erep.shape)],
        out_specs=pl.BlockSpec((rows, dvt), lambda b, c: (row(b, c), 0)),
        out_shape=jax.ShapeDtypeStruct((t, dvt), BF16),
        scratch_shapes=[pltpu.VMEM((GLA_HEADS, dk, dv), F32),
                        pltpu.VMEM((GLA_HEADS, GLA_CHUNK, dk), F32),
                        pltpu.VMEM((GLA_HEADS, GLA_CHUNK, dk), F32),
                        pltpu.VMEM((GLA_HEADS, GLA_CHUNK, GLA_SUB * dk), BF16)],
        compiler_params=_cparams(2), name="gla_core",
    )(qkvg, qkvg, qkvg, qkvg, la, gnorm.reshape(1, dv), amat, erep)


def _sba_kernel(q_ref, k_ref, v_ref, u_ref, o_ref, qs_s, acc_s, carry_s, *, scale, dh):
    bq = SBA_BQ
    nq = q_ref.shape[0] // bq
    step = pl.program_id(2)
    lane = lax.broadcasted_iota(I32, (bq, LANE), 1)
    row = lax.broadcasted_iota(I32, (2 * bq, bq), 0)
    strict = lax.broadcasted_iota(I32, (2 * bq, bq), 1) < jnp.where(row >= bq, row - bq, row)

    def block(jb, diagonal):
        k0 = pl.multiple_of(jb * bq, bq)
        kb = k_ref[pl.ds(k0, bq), :]
        vb = v_ref[pl.ds(k0, bq), :]
        z = lax.dot_general(qs_s[...], kb, (((1,), (1,)), ((), ())), preferred_element_type=F32)
        sp = jnp.maximum(z, 0.0) + jnp.log2(1.0 + jnp.exp2(jnp.minimum(z, -z)))
        if diagonal:
            sp = jnp.where(strict, sp, 0.0)
        ct = jnp.dot(sp.astype(BF16), u_ref[...], preferred_element_type=F32)
        carry = carry_s[...]
        w = jnp.exp2((z - sp) - ct - jnp.concatenate([carry] * (bq // LANE), axis=1))
        if diagonal:
            w = jnp.where(strict, w, 0.0)
        acc_s[...] += jnp.dot(w.astype(BF16), vb, preferred_element_type=F32)
        carry = carry + jnp.broadcast_to(ct[:, 0:1] + sp[:, 0:1], carry.shape)
        carry_s[...] = carry
        return jnp.min(carry)

    def qblock(qi, c):
        r0 = pl.multiple_of(qi * bq, bq)
        q2 = q_ref[pl.ds(r0, bq), :].astype(F32) * scale
        qs_s[0:bq] = jnp.where(lane < dh, q2, 0.0).astype(BF16)
        qs_s[bq:2 * bq] = jnp.where(lane >= dh, q2, 0.0).astype(BF16)
        acc_s[...] = jnp.zeros_like(acc_s)
        carry_s[...] = jnp.zeros_like(carry_s)
        jd = step * nq + qi
        low = block(jd, True)
        lax.while_loop(lambda st: jnp.logical_and(st[0] >= 0, st[1] < SBA_DEAD2),
                       lambda st: (st[0] - 1, block(st[0], False)), (jd - 1, low))
        a = acc_s[...]
        o_ref[pl.ds(r0, bq), :] = jnp.where(lane < dh, a[:bq], a[bq:]).astype(o_ref.dtype)
        return c

    lax.fori_loop(0, nq, qblock, 0)


def sba_core(qkv, batch, seq):
    t = qkv.shape[0]
    d = qkv.shape[1] // 3
    dh = d // SBA_HEADS
    npair = d // LANE
    bq = SBA_BQ
    rows = min(SBA_ROWS, seq)
    nstep = seq // rows
    u = np.tril(np.ones((bq, bq), np.float32), -1)
    return pl.pallas_call(
        functools.partial(_sba_kernel, scale=float(dh) ** -0.5 * LOG2E, dh=dh),
        grid=(batch, npair, nstep),
        in_specs=[pl.BlockSpec((rows, LANE), lambda b, p, i: (b * nstep + i, p)),
                  pl.BlockSpec((seq, LANE), lambda b, p, i: (b, npair + p)),
                  pl.BlockSpec((seq, LANE), lambda b, p, i: (b, 2 * npair + p)),
                  _const_spec((bq, bq))],
        out_specs=pl.BlockSpec((rows, LANE), lambda b, p, i: (b * nstep + i, p)),
        out_shape=jax.ShapeDtypeStruct((t, d), BF16),
        scratch_shapes=[pltpu.VMEM((2 * bq, LANE), BF16), pltpu.VMEM((2 * bq, LANE), F32),
                        pltpu.VMEM((2 * bq, LANE), F32)],
        compiler_params=_cparams(3), name="sba_core",
    )(qkv, qkv, qkv, jnp.asarray(u, dtype=BF16))


def _pool_kernel(x_ref, halo_ref, g_ref, win_ref, wgrp_ref, sc_ref, o_ref, u_s, *, tiles_per_seq):
    tm, d = x_ref.shape
    hal = halo_ref.shape[0]
    gd = wgrp_ref.shape[1]
    first = (pl.program_id(0) % tiles_per_seq) == 0
    x = x_ref[...]
    xx = jnp.concatenate([halo_ref[...], x], axis=0)
    h = _rms(xx, g_ref[...]).astype(BF16)
    u = jnp.dot(h, win_ref[...], preferred_element_type=F32)
    keep = jnp.where(first, 0.0, 1.0)
    rowi = lax.broadcasted_iota(I32, (hal + tm, 1), 0)
    u_s[...] = u * jnp.where(rowi < hal, keep, 1.0)
    pos = (pl.program_id(0) % tiles_per_seq) * tm + lax.broadcasted_iota(I32, (tm, 1), 0)
    mixed = []
    for gi, w in enumerate(POOL_WINDOWS):
        cs = slice(gi * gd, (gi + 1) * gd)
        ssum = u_s[hal:hal + tm, cs]
        for dlt in range(1, w):
            ssum = ssum + u_s[hal - dlt:hal - dlt + tm, cs]
        cnt = jnp.minimum(pos + 1, w).astype(F32)
        pooled = ssum / cnt - u_s[hal:hal + tm, cs]
        mixed.append(jnp.dot(pooled.astype(BF16), wgrp_ref[gi], preferred_element_type=F32))
    o_ref[...] = x + jnp.concatenate(mixed, axis=1) * sc_ref[...]


def pool_mixer(x, gain, w_in, w_group, scale, seq, tm=ROW_TILE):
    t, d = x.shape
    hal = max(POOL_WINDOWS)
    g, gd, _ = w_group.shape
    return pl.pallas_call(
        functools.partial(_pool_kernel, tiles_per_seq=seq // tm),
        grid=(t // tm,),
        in_specs=[pl.BlockSpec((tm, d), lambda i: (i, 0)),
                  pl.BlockSpec((hal, d), lambda i: (jnp.maximum(i * (tm // hal) - 1, 0), 0)),
                  _const_spec((1, d)), _const_spec((d, d)), _const_spec((g, gd, gd)),
                  _const_spec((1, d))],
        out_specs=pl.BlockSpec((tm, d), lambda i: (i, 0)),
        out_shape=jax.ShapeDtypeStruct((t, d), F32),
        scratch_shapes=[pltpu.VMEM((hal + tm, d), F32)],
        compiler_params=_cparams(1), name="pool_mixer",
    )(x, x, gain.reshape(1, d), w_in, w_group, scale.reshape(1, d))


def _swiglu_acc(h, wg_ref, wu_ref, wd_ref, acc, widx):
    ff = wg_ref.shape[-1]
    for c0 in range(0, ff, FF_CHUNK):
        cs = slice(c0, c0 + FF_CHUNK)
        gate = jnp.dot(h, wg_ref[widx + (slice(None), cs)], preferred_element_type=F32)
        up = jnp.dot(h, wu_ref[widx + (slice(None), cs)], preferred_element_type=F32)
        a = (gate * _sigmoid(gate) * up).astype(BF16)
        acc = acc + jnp.dot(a, wd_ref[widx + (cs, slice(None))], preferred_element_type=F32)
    return acc


def _ffn_kernel(x_ref, a_ref, wo_ref, g_ref, wg_ref, wu_ref, wd_ref, o_ref):
    x = x_ref[...] + jnp.dot(a_ref[...], wo_ref[...], preferred_element_type=F32)
    h = _rms(x, g_ref[...]).astype(BF16)
    o_ref[...] = _swiglu_acc(h, wg_ref, wu_ref, wd_ref, x, ())


def _ffn_plain_kernel(x_ref, g_ref, wg_ref, wu_ref, wd_ref, o_ref):
    x = x_ref[...]
    h = _rms(x, g_ref[...]).astype(BF16)
    o_ref[...] = _swiglu_acc(h, wg_ref, wu_ref, wd_ref, x, ())


def ffn_dense(x, proj, gain, wg, wu, wd, tm=ROW_TILE):
    t, d = x.shape
    ff = wg.shape[1]
    row_spec = pl.BlockSpec((tm, d), lambda i: (i, 0))
    w_specs = [_const_spec((1, d)), _const_spec((d, ff)), _const_spec((d, ff)), _const_spec((ff, d))]
    if proj is None:
        body, in_specs, args = _ffn_plain_kernel, [row_spec], (x,)
    else:
        a, w_out = proj
        ka = a.shape[1]
        body, args = _ffn_kernel, (x, a, w_out)
        in_specs = [row_spec, pl.BlockSpec((tm, ka), lambda i: (i, 0)), _const_spec((ka, d))]
    return pl.pallas_call(
        body,
        grid=(t // tm,),
        in_specs=in_specs + w_specs,
        out_specs=row_spec,
        out_shape=jax.ShapeDtypeStruct((t, d), F32),
        compiler_params=_cparams(1), name="ffn_dense",
    )(*args, gain.reshape(1, d), wg, wu, wd)


def _moe_ffn_kernel(texp_ref, nused_ref, x_ref, g_ref, wg_ref, wu_ref, wd_ref, o_ref):
    live = pl.program_id(0) < nused_ref[0]

    @pl.when(live)
    def _():
        h = _rms(x_ref[...], g_ref[...]).astype(BF16)
        o_ref[...] = _swiglu_acc(h, wg_ref, wu_ref, wd_ref, jnp.zeros(o_ref.shape, F32), (0,))

    @pl.when(jnp.logical_not(live))
    def _():
        o_ref[...] = jnp.zeros_like(o_ref)


def moe_ffn(xs, gain, wg, wu, wd, tile_expert, n_used, tm):
    r, d = xs.shape
    ff = wg.shape[2]
    live = lambda i, te, nu: jnp.minimum(i, nu[0] - 1)
    grid_spec = pltpu.PrefetchScalarGridSpec(
        num_scalar_prefetch=2,
        grid=(r // tm,),
        in_specs=[pl.BlockSpec((tm, d), lambda i, te, nu: (live(i, te, nu), 0)),
                  pl.BlockSpec((1, d), lambda i, te, nu: (0, 0)),
                  pl.BlockSpec((1, d, ff), lambda i, te, nu: (te[i], 0, 0)),
                  pl.BlockSpec((1, d, ff), lambda i, te, nu: (te[i], 0, 0)),
                  pl.BlockSpec((1, ff, d), lambda i, te, nu: (te[i], 0, 0))],
        out_specs=pl.BlockSpec((tm, d), lambda i, te, nu: (i, 0)),
    )
    return pl.pallas_call(
        _moe_ffn_kernel, grid_spec=grid_spec,
        out_shape=jax.ShapeDtypeStruct((r, d), F32),
        compiler_params=_cparams(1), name="moe_ffn",
    )(tile_expert, n_used, xs, gain.reshape(1, d), wg, wu, wd)


def _router_kernel(x_ref, a_ref, wo_ref, g_ref, wr_ref, xo_ref, idx_ref, prob_ref, *, ne):
    x = x_ref[...] + jnp.dot(a_ref[...], wo_ref[...], preferred_element_type=F32)
    xo_ref[...] = x
    h = _rms(x, g_ref[...])
    logits = jnp.dot(h, wr_ref[...], precision=HIGHEST, preferred_element_type=F32)
    col = lax.broadcasted_iota(I32, logits.shape, 1).astype(F32)
    logits = jnp.where(col < ne, logits, -jnp.inf)
    m1 = jnp.max(logits, axis=1, keepdims=True)
    i1 = jnp.min(jnp.where(logits == m1, col, float(ne)), axis=1, keepdims=True)
    rest = jnp.where(col == i1, -jnp.inf, logits)
    m2 = jnp.max(rest, axis=1, keepdims=True)
    i2 = jnp.min(jnp.where(rest == m2, col, float(ne)), axis=1, keepdims=True)
    e2 = jnp.exp(m2 - m1)
    p1 = 1.0 / (1.0 + e2)
    idx_ref[...] = jnp.where(col == 0.0, i1, i2).astype(I32)
    prob_ref[...] = jnp.where(col == 0.0, p1, e2 * p1)


def router_top2(x, a, w_out, gain, wr, tm=ROW_TILE):
    t, d = x.shape
    ka = a.shape[1]
    ne = wr.shape[1]
    wr_pad = jnp.zeros((d, LANE), F32).at[:, :ne].set(wr)
    return pl.pallas_call(
        functools.partial(_router_kernel, ne=ne),
        grid=(t // tm,),
        in_specs=[pl.BlockSpec((tm, d), lambda i: (i, 0)), pl.BlockSpec((tm, ka), lambda i: (i, 0)),
                  _const_spec((ka, d)), _const_spec((1, d)), _const_spec((d, LANE))],
        out_specs=[pl.BlockSpec((tm, d), lambda i: (i, 0)),
                   pl.BlockSpec((tm, LANE), lambda i: (i, 0)), pl.BlockSpec((tm, LANE), lambda i: (i, 0))],
        out_shape=[jax.ShapeDtypeStruct((t, d), F32),
                   jax.ShapeDtypeStruct((t, LANE), I32), jax.ShapeDtypeStruct((t, LANE), F32)],
        compiler_params=_cparams(1), name="router_top2",
    )(x, a, w_out, gain.reshape(1, d), wr_pad)


def _dispatch_kernel(pos_ref, x_ref, xs_in, xs_hbm, sem):
    del xs_in
    tm = x_ref.shape[0]

    def issue(k, c):
        pltpu.make_async_copy(x_ref.at[pl.ds(k, 1)], xs_hbm.at[pl.ds(pos_ref[0, 0, k], 1)], sem).start()
        pltpu.make_async_copy(x_ref.at[pl.ds(k, 1)], xs_hbm.at[pl.ds(pos_ref[0, 1, k], 1)], sem).start()
        return c

    lax.fori_loop(0, tm, issue, 0, unroll=8)
    pltpu.make_async_copy(x_ref, xs_hbm.at[pl.ds(0, tm)], sem).wait()
    pltpu.make_async_copy(x_ref, xs_hbm.at[pl.ds(0, tm)], sem).wait()


def dispatch_rows(x, pos, n_rows, tm=DISPATCH_ROWS):
    t, d = x.shape
    pos3 = pos.reshape(t // tm, tm, 2).transpose(0, 2, 1)
    return pl.pallas_call(
        _dispatch_kernel,
        grid=(t // tm,),
        in_specs=[pl.BlockSpec((1, 2, tm), lambda i: (i, 0, 0), memory_space=pltpu.SMEM),
                  pl.BlockSpec((tm, d), lambda i: (i, 0)),
                  pl.BlockSpec(memory_space=pl.ANY)],
        out_specs=pl.BlockSpec(memory_space=pl.ANY),
        out_shape=jax.ShapeDtypeStruct((n_rows, d), x.dtype),
        input_output_aliases={2: 0},
        scratch_shapes=[pltpu.SemaphoreType.DMA(())],
        compiler_params=_cparams(1), name="dispatch_rows",
    )(pos3, x, jnp.zeros((n_rows, d), x.dtype))


def _combine_kernel(pos_ref, x_ref, p_ref, g_ref, y_hbm, o_ref, buf, sem, *, final_norm):
    tm = x_ref.shape[0]

    def issue(k, c):
        pltpu.make_async_copy(y_hbm.at[pl.ds(pos_ref[0, 0, k], 1)], buf.at[0, pl.ds(k, 1)], sem).start()
        pltpu.make_async_copy(y_hbm.at[pl.ds(pos_ref[0, 1, k], 1)], buf.at[1, pl.ds(k, 1)], sem).start()
        return c

    lax.fori_loop(0, tm, issue, 0, unroll=8)
    pltpu.make_async_copy(y_hbm.at[pl.ds(0, tm)], buf.at[0], sem).wait()
    pltpu.make_async_copy(y_hbm.at[pl.ds(0, tm)], buf.at[1], sem).wait()
    p = p_ref[...]
    out = x_ref[...] + p[:, 0:1] * buf[0] + p[:, 1:2] * buf[1]
    if final_norm:
        out = _rms(out, g_ref[...])
    o_ref[...] = out


def moe_combine(x, ys, pos, prob, final_gain, tm=256):
    t, d = x.shape
    final_norm = final_gain is not None
    gain = final_gain if final_norm else jnp.ones((d,), F32)
    pos3 = pos.reshape(t // tm, tm, 2).transpose(0, 2, 1)
    return pl.pallas_call(
        functools.partial(_combine_kernel, final_norm=final_norm),
        grid=(t // tm,),
        in_specs=[pl.BlockSpec((1, 2, tm), lambda i: (i, 0, 0), memory_space=pltpu.SMEM),
                  pl.BlockSpec((tm, d), lambda i: (i, 0)),
                  pl.BlockSpec((tm, LANE), lambda i: (i, 0)),
                  _const_spec((1, d)),
                  pl.BlockSpec(memory_space=pl.ANY)],
        out_specs=pl.BlockSpec((tm, d), lambda i: (i, 0)),
        out_shape=jax.ShapeDtypeStruct((t, d), F32),
        scratch_shapes=[pltpu.VMEM((2, tm, d), F32), pltpu.SemaphoreType.DMA(())],
        compiler_params=_cparams(1), name="moe_combine",
    )(pos3, x, prob, gain.reshape(1, d), ys)


def _final_norm_kernel(x_ref, g_ref, o_ref):
    o_ref[...] = _rms(x_ref[...], g_ref[...])


def final_norm(x, gain, tm=ROW_TILE):
    t, d = x.shape
    return pl.pallas_call(
        _final_norm_kernel, grid=(t // tm,),
        in_specs=[pl.BlockSpec((tm, d), lambda i: (i, 0)), _const_spec((1, d))],
        out_specs=pl.BlockSpec((tm, d), lambda i: (i, 0)),
        out_shape=jax.ShapeDtypeStruct((t, d), F32),
        compiler_params=_cparams(1), name="final_norm",
    )(x, gain.reshape(1, d))


def _route_plan(idx, tm):
    t = idx.shape[0]
    e_flat = idx.reshape(-1)
    onehot = (e_flat[:, None] == jnp.arange(N_EXPERTS, dtype=I32)[None, :]).astype(I32)
    csum = jnp.cumsum(onehot, axis=0)
    rank = jnp.take_along_axis(csum, e_flat[:, None], axis=1)[:, 0] - 1
    counts = csum[-1]
    padded = ((counts + tm - 1) // tm) * tm
    ends = jnp.cumsum(padded)
    pos = (ends - padded)[e_flat] + rank
    n_rows = 2 * t + N_EXPERTS * tm
    n_used = (ends[-1] // tm).astype(I32)
    tile_start = jnp.arange(n_rows // tm, dtype=I32) * tm
    tile_start = jnp.minimum(tile_start, ends[-1] - 1)
    tile_expert = jnp.sum((ends[None, :] <= tile_start[:, None]).astype(I32), axis=1)
    tile_expert = jnp.minimum(tile_expert, N_EXPERTS - 1)
    return pos.reshape(t, 2), tile_expert, n_used.reshape(1), n_rows


def moe_layer(x, proj, gain, router, wg, wu, wd, final_gain, tm=MOE_TILE, tc=DISPATCH_ROWS):
    x, idx, prob = router_top2(x, proj[0], proj[1], gain, router)
    pos, tile_expert, n_used, n_rows = _route_plan(idx[:, :2], tm)
    xs = dispatch_rows(x, pos, n_rows, tm=tc)
    ys = moe_ffn(xs, gain, wg, wu, wd, tile_expert, n_used, tm)
    return moe_combine(x, ys, pos, prob, final_gain, tm=tc)


def gla_layer(x, gain, w_in, w_gate_up, b_gate, gnorm, w_out, batch, seq):
    d = x.shape[1]
    n_main = w_in.shape[1] - GLA_GATE_RANK
    w_all = jnp.zeros((d, n_main + LANE), BF16).at[:, :w_in.shape[1]].set(w_in.astype(BF16))
    w_gu = jnp.zeros((LANE, w_gate_up.shape[1]), F32).at[:GLA_GATE_RANK].set(w_gate_up)
    qkvg, la = gla_proj(x, gain, w_all, w_gu, b_gate)
    og = gla_core(qkvg, la, gnorm, batch, seq, rows=min(GLA_ROWS, seq))
    return og, w_out.astype(BF16)


def sba_layer(x, gain, w_in, w_out, batch, seq):
    qkv = norm_proj(x, gain, w_in.astype(BF16))
    return sba_core(qkv, batch, seq), w_out.astype(BF16)


def kernel(x, norm_mix, norm_ffn, norm_final, gla_w_in, gla_w_gate_up, gla_b_gate, gla_norm, gla_w_out, sba_w_in, sba_w_out, pool_w_in, pool_w_group, pool_scale, ffn_w_gate, ffn_w_up, ffn_w_down, moe_router, moe_w_gate, moe_w_up, moe_w_down):
    batch, seq, d = x.shape
    depth = norm_mix.shape[0]
    x = x.reshape(batch * seq, d)
    normed = False
    for i in range(depth):
        kind, j = i % 3, i // 3
        if kind == 0:
            proj = gla_layer(x, norm_mix[i], gla_w_in[j], gla_w_gate_up[j], gla_b_gate[j],
                             gla_norm[j], gla_w_out[j], batch, seq)
        elif kind == 1:
            proj = sba_layer(x, norm_mix[i], sba_w_in[j], sba_w_out[j], batch, seq)
        else:
            proj = None
            x = pool_mixer(x, norm_mix[i], pool_w_in[j].astype(BF16), pool_w_group[j].astype(BF16),
                           pool_scale[j], seq)
        f = i // 2
        if i % 2 == 0:
            x = ffn_dense(x, proj, norm_ffn[i], ffn_w_gate[f].astype(BF16), ffn_w_up[f].astype(BF16),
                          ffn_w_down[f].astype(BF16))
        else:
            if proj is None:
                raise NotImplementedError("pooling mixer followed by the routed block")
            last = i == depth - 1
            x = moe_layer(x, proj, norm_ffn[i], moe_router[f], moe_w_gate[f].astype(BF16),
                          moe_w_up[f].astype(BF16), moe_w_down[f].astype(BF16),
                          norm_final if last else None)
            normed = last
    if not normed:
        x = final_norm(x, norm_final)
    return x.reshape(batch, seq, d)
```

```python
import functools

import numpy as np
import jax
import jax.numpy as jnp
from jax import lax
from jax.experimental import pallas as pl
from jax.experimental.pallas import tpu as pltpu

F32 = jnp.float32
BF16 = jnp.bfloat16
I32 = jnp.int32
HIGHEST = lax.Precision.HIGHEST

RMS_EPS = 1e-6
GLA_HEADS = 4
GLA_GATE_RANK = 16
GLA_GATE_NORMALIZER = 16.0
SBA_HEADS = 16
POOL_WINDOWS = (2, 4, 8, 16)
N_EXPERTS = 8

LANE = 128
VMEM_LIMIT = 56 * 1024 * 1024
ROW_TILE = 512
FF_CHUNK = 256
GLA_CHUNK = 64
GLA_SUB = 16
GLA_ROWS = 512
GLA_SAFE_EXP = 60.0
SBA_BQ = 256
SBA_ROWS = 1024
SBA_DEAD2 = 150.0
LOG2E = 1.4426950408889634
MOE_TILE = 512
DISPATCH_ROWS = 256


def _cparams(n_axes):
    return pltpu.CompilerParams(dimension_semantics=("arbitrary",) * n_axes,
                                vmem_limit_bytes=VMEM_LIMIT)


def _rms(xf, gain):
    return xf * lax.rsqrt(jnp.mean(xf * xf, axis=-1, keepdims=True) + RMS_EPS) * gain


def _sigmoid(x):
    return 1.0 / (1.0 + jnp.exp(-x))


def _log_sigmoid(z):
    return jnp.minimum(z, 0.0) - jnp.log(1.0 + jnp.exp(-jnp.abs(z)))


def _dot_split(a, b, split_lhs=False):
    f = a if split_lhs else b
    hi = f.astype(BF16)
    lo = (f - hi.astype(F32)).astype(BF16)
    if split_lhs:
        return (jnp.dot(hi, b, preferred_element_type=F32) + jnp.dot(lo, b, preferred_element_type=F32))
    return (jnp.dot(a, hi, preferred_element_type=F32) + jnp.dot(a, lo, preferred_element_type=F32))


def _const_spec(shape):
    return pl.BlockSpec(shape, lambda *_: (0,) * len(shape))


def _norm_proj_kernel(x_ref, g_ref, w_ref, o_ref, *, nchunk):
    h = _rms(x_ref[...], g_ref[...]).astype(BF16)
    n = o_ref.shape[1]
    for c0 in range(0, n, nchunk):
        o_ref[:, c0:c0 + nchunk] = jnp.dot(
            h, w_ref[:, c0:c0 + nchunk], preferred_element_type=F32).astype(o_ref.dtype)


def norm_proj(x, gain, w, tm=ROW_TILE):
    t, d = x.shape
    n = w.shape[1]
    return pl.pallas_call(
        functools.partial(_norm_proj_kernel, nchunk=512),
        grid=(t // tm,),
        in_specs=[pl.BlockSpec((tm, d), lambda i: (i, 0)),
                  _const_spec((1, d)), _const_spec((d, n))],
        out_specs=pl.BlockSpec((tm, n), lambda i: (i, 0)),
        out_shape=jax.ShapeDtypeStruct((t, n), BF16),
        compiler_params=_cparams(1), name="norm_proj",
    )(x, gain.reshape(1, d), w)


def _gla_proj_kernel(x_ref, g_ref, w_ref, wgu_ref, bg_ref, o_ref, la_ref, *, nchunk):
    h = _rms(x_ref[...], g_ref[...]).astype(BF16)
    n = o_ref.shape[1]
    for c0 in range(0, n, nchunk):
        o_ref[:, c0:c0 + nchunk] = jnp.dot(
            h, w_ref[:, c0:c0 + nchunk], preferred_element_type=F32).astype(o_ref.dtype)
    a = jnp.dot(h, w_ref[:, n:], preferred_element_type=F32)
    z = jnp.dot(a, wgu_ref[...], precision=HIGHEST, preferred_element_type=F32) + bg_ref[...]
    la_ref[...] = _log_sigmoid(z) / GLA_GATE_NORMALIZER


def gla_proj(x, gain, w_all, w_gu, b_gate, tm=ROW_TILE):
    t, d = x.shape
    n = w_all.shape[1] - LANE
    dk = w_gu.shape[1]
    return pl.pallas_call(
        functools.partial(_gla_proj_kernel, nchunk=512),
        grid=(t // tm,),
        in_specs=[pl.BlockSpec((tm, d), lambda i: (i, 0)),
                  _const_spec((1, d)), _const_spec((d, n + LANE)),
                  _const_spec((LANE, dk)), _const_spec((1, dk))],
        out_specs=[pl.BlockSpec((tm, n), lambda i: (i, 0)),
                   pl.BlockSpec((tm, dk), lambda i: (i, 0))],
        out_shape=[jax.ShapeDtypeStruct((t, n), BF16), jax.ShapeDtypeStruct((t, dk), F32)],
        compiler_params=_cparams(1), name="gla_proj",
    )(x, gain.reshape(1, d), w_all, w_gu, b_gate.reshape(1, dk))


def _gla_tables():
    c, s = GLA_CHUNK, GLA_SUB
    i = np.arange(c)[:, None]
    j = np.arange(c)[None, :]
    lo = (i // s) * s
    hi = lo + s
    a1 = (j >= lo) & (j <= i)
    a2 = (j > i) & (j < hi)
    a3 = j < lo
    a4 = j >= hi
    amat = np.concatenate([a1, a2, a3, a4], axis=0).astype(np.float32)
    row = np.arange(s * LANE)[:, None] // LANE
    erep = (row == (np.arange(c)[None, :] % s)).astype(np.float32)
    return jnp.asarray(amat, dtype=BF16), jnp.asarray(erep, dtype=BF16)


def _gla_kernel(q_ref, k_ref, v_ref, g_ref, la_ref, gn_ref, amat_ref, erep_ref,
                og_ref, state_s, kf_s, dq_s, tm_s, *, qscale):
    c, s = GLA_CHUNK, GLA_SUB
    nsub = c // s
    nh = state_s.shape[0]
    dk = q_ref.shape[1] // nh
    dv = v_ref.shape[1] // nh

    @pl.when(pl.program_id(1) == 0)
    def _():
        state_s[...] = jnp.zeros_like(state_s)

    il = lax.broadcasted_iota(I32, (s, dk), 0)
    ii = lax.broadcasted_iota(I32, (c, c), 0)
    jj = lax.broadcasted_iota(I32, (c, c), 1)
    ri, rj = ii // s, jj // s
    on_diag = jnp.logical_and(ri == rj, ii >= jj)
    rowblk = lax.broadcasted_iota(I32, (c, dk), 0) // s
    ones_ck = jnp.ones((c, dk), BF16)

    def head_chunk(h, r0, exact_diag):
        kcols = slice(h * dk, (h + 1) * dk)
        vcols = slice(h * dv, (h + 1) * dv)
        la = la_ref[pl.ds(r0, c), kcols]
        qf = q_ref[pl.ds(r0, c), kcols].astype(F32) * qscale
        kf = k_ref[pl.ds(r0, c), kcols].astype(F32)
        vb = v_ref[pl.ds(r0, c), vcols]
        cum = _dot_split(amat_ref[...], la)
        dq, dkk, cprev, ctail = cum[0:c], cum[c:2 * c], cum[2 * c:3 * c], cum[3 * c:4 * c]
        cnext = cprev + dq + dkk
        qt = qf * jnp.exp(dq)
        kt = kf * jnp.exp(dkk)
        if exact_diag:
            kf_s[h] = kf
            dq_s[h] = dq
            for r in range(nsub):
                qb = qf[r * s:(r + 1) * s]
                dqb = dq[r * s:(r + 1) * s]
                for jl in range(s):
                    krow = kf_s[h, pl.ds(r * s + jl, 1), :]
                    drow = dq_s[h, pl.ds(r * s + jl, 1), :]
                    dec = jnp.where(il >= jl, jnp.exp(jnp.minimum(dqb - drow, 0.0)), 0.0)
                    tm_s[h, r * s:(r + 1) * s, jl * dk:(jl + 1) * dk] = (qb * krow * dec).astype(BF16)
            sdiag = jnp.dot(tm_s[h], erep_ref[...], preferred_element_type=F32)
        else:
            kd = (kf * jnp.exp(-dq)).astype(BF16)
            sdiag = lax.dot_general(qt.astype(BF16), kd, (((1,), (1,)), ((), ())),
                                    preferred_element_type=F32)
        soff = jnp.zeros((c, c), F32)
        for rp in range(nsub - 1):
            cn = cnext[rp * s:rp * s + 1, :]
            p = (qt * jnp.exp(jnp.minimum(cprev - cn, 0.0))).astype(BF16)
            kp = jnp.where(rowblk == rp, kt, 0.0).astype(BF16)
            soff = soff + lax.dot_general(p, kp, (((1,), (1,)), ((), ())),
                                          preferred_element_type=F32)
        sc = jnp.where(on_diag, sdiag, jnp.where(ri > rj, soff, 0.0))
        st = state_s[h]
        o = jnp.dot(sc.astype(BF16), vb, preferred_element_type=F32) + jnp.dot(
            (qt * jnp.exp(cprev)).astype(BF16), st.astype(BF16), preferred_element_type=F32)
        ks_t = (kt * jnp.exp(ctail)).T.astype(BF16)
        upd = jnp.dot(ks_t, vb, preferred_element_type=F32)
        dcol = jnp.exp(_dot_split(la.T, ones_ck, split_lhs=True))
        state_s[h] = jnp.concatenate([dcol] * (dv // dk), axis=1) * st + upd
        gf = g_ref[pl.ds(r0, c), vcols].astype(F32)
        og_ref[pl.ds(r0, c), vcols] = (_rms(o, gn_ref[...]) * (gf * _sigmoid(gf))).astype(og_ref.dtype)

    def sweep(exact_diag):
        def chunk(ci, carry):
            r0 = pl.multiple_of(ci * c, c)
            for h in range(nh):
                head_chunk(h, r0, exact_diag)
            return carry

        lax.fori_loop(0, q_ref.shape[0] // c, chunk, 0, unroll=1 if exact_diag else 2)

    safe = jnp.max(-la_ref[...]) * s < GLA_SAFE_EXP
    pl.when(safe)(lambda: sweep(False))
    pl.when(jnp.logical_not(safe))(lambda: sweep(True))


def gla_core(qkvg, la, gnorm, batch, seq, rows=GLA_ROWS):
    t = qkvg.shape[0]
    dkt = la.shape[1]
    dk = dkt // GLA_HEADS
    dvt = (qkvg.shape[1] - 2 * dkt) // 2
    dv = dvt // GLA_HEADS
    nblk = seq // rows
    amat, erep = _gla_tables()
    row = lambda b, c: b * nblk + c
    return pl.pallas_call(
        functools.partial(_gla_kernel, qscale=float(dk) ** -0.5),
        grid=(batch, nblk),
        in_specs=[pl.BlockSpec((rows, dkt), lambda b, c: (row(b, c), 0)),
                  pl.BlockSpec((rows, dkt), lambda b, c: (row(b, c), 1)),
                  pl.BlockSpec((rows, dvt), lambda b, c: (row(b, c), 2 * dkt // dvt)),
                  pl.BlockSpec((rows, dvt), lambda b, c: (row(b, c), 2 * dkt // dvt + 1)),
                  pl.BlockSpec((rows, dkt), lambda b, c: (row(b, c), 0)),
                  _const_spec((1, dv)), _const_spec(amat.shape), _const_spec(erep.shape)],
        out_specs=pl.BlockSpec((rows, dvt), lambda b, c: (row(b, c), 0)),
        out_shape=jax.ShapeDtypeStruct((t, dvt), BF16),
        scratch_shapes=[pltpu.VMEM((GLA_HEADS, dk, dv), F32),
                        pltpu.VMEM((GLA_HEADS, GLA_CHUNK, dk), F32),
                        pltpu.VMEM((GLA_HEADS, GLA_CHUNK, dk), F32),
                        pltpu.VMEM((GLA_HEADS, GLA_CHUNK, GLA_SUB * dk), BF16)],
        compiler_params=_cparams(2), name="gla_core",
    )(qkvg, qkvg, qkvg, qkvg, la, gnorm.reshape(1, dv), amat, erep)


def _sba_kernel(q_ref, k_ref, v_ref, u_ref, o_ref, qs_s, acc_s, carry_s, *, scale, dh):
    bq = SBA_BQ
    nq = q_ref.shape[0] // bq
    step = pl.program_id(2)
    lane = lax.broadcasted_iota(I32, (bq, LANE), 1)
    row = lax.broadcasted_iota(I32, (2 * bq, bq), 0)
    strict = lax.broadcasted_iota(I32, (2 * bq, bq), 1) < jnp.where(row >= bq, row - bq, row)

    def block(jb, diagonal):
        k0 = pl.multiple_of(jb * bq, bq)
        kb = k_ref[pl.ds(k0, bq), :]
        vb = v_ref[pl.ds(k0, bq), :]
        z = lax.dot_general(qs_s[...], kb, (((1,), (1,)), ((), ())), preferred_element_type=F32)
        sp = jnp.maximum(z, 0.0) + jnp.log2(1.0 + jnp.exp2(jnp.minimum(z, -z)))
        if diagonal:
            sp = jnp.where(strict, sp, 0.0)
        ct = jnp.dot(sp.astype(BF16), u_ref[...], preferred_element_type=F32)
        carry = carry_s[...]
        w = jnp.exp2((z - sp) - ct[:, :bq] - jnp.concatenate([carry] * (bq // LANE), axis=1))
        if diagonal:
            w = jnp.where(strict, w, 0.0)
        acc_s[...] += jnp.dot(w.astype(BF16), vb, preferred_element_type=F32)
        carry = carry + ct[:, bq:]
        carry_s[...] = carry
        return jnp.min(carry)

    def qblock(qi, c):
        r0 = pl.multiple_of(qi * bq, bq)
        q2 = q_ref[pl.ds(r0, bq), :].astype(F32) * scale
        qs_s[0:bq] = jnp.where(lane < dh, q2, 0.0).astype(BF16)
        qs_s[bq:2 * bq] = jnp.where(lane >= dh, q2, 0.0).astype(BF16)
        acc_s[...] = jnp.zeros_like(acc_s)
        carry_s[...] = jnp.zeros_like(carry_s)
        jd = step * nq + qi
        low = block(jd, True)
        lax.while_loop(lambda st: jnp.logical_and(st[0] >= 0, st[1] < SBA_DEAD2),
                       lambda st: (st[0] - 1, block(st[0], False)), (jd - 1, low))
        a = acc_s[...]
        o_ref[pl.ds(r0, bq), :] = jnp.where(lane < dh, a[:bq], a[bq:]).astype(o_ref.dtype)
        return c

    lax.fori_loop(0, nq, qblock, 0)


def sba_core(qkv, batch, seq):
    t = qkv.shape[0]
    d = qkv.shape[1] // 3
    dh = d // SBA_HEADS
    npair = d // LANE
    bq = SBA_BQ
    rows = min(SBA_ROWS, seq)
    nstep = seq // rows
    u = np.concatenate([np.tril(np.ones((bq, bq), np.float32), -1), np.ones((bq, LANE), np.float32)], 1)
    return pl.pallas_call(
        functools.partial(_sba_kernel, scale=float(dh) ** -0.5 * LOG2E, dh=dh),
        grid=(batch, npair, nstep),
        in_specs=[pl.BlockSpec((rows, LANE), lambda b, p, i: (b * nstep + i, p)),
                  pl.BlockSpec((seq, LANE), lambda b, p, i: (b, npair + p)),
                  pl.BlockSpec((seq, LANE), lambda b, p, i: (b, 2 * npair + p)),
                  _const_spec((bq, bq + LANE))],
        out_specs=pl.BlockSpec((rows, LANE), lambda b, p, i: (b * nstep + i, p)),
        out_shape=jax.ShapeDtypeStruct((t, d), BF16),
        scratch_shapes=[pltpu.VMEM((2 * bq, LANE), BF16), pltpu.VMEM((2 * bq, LANE), F32),
                        pltpu.VMEM((2 * bq, LANE), F32)],
        compiler_params=_cparams(3), name="sba_core",
    )(qkv, qkv, qkv, jnp.asarray(u, dtype=BF16))


def _pool_kernel(x_ref, halo_ref, g_ref, win_ref, wgrp_ref, sc_ref, o_ref, u_s, *, tiles_per_seq):
    tm, d = x_ref.shape
    hal = halo_ref.shape[0]
    gd = wgrp_ref.shape[1]
    first = (pl.program_id(0) % tiles_per_seq) == 0
    x = x_ref[...]
    xx = jnp.concatenate([halo_ref[...], x], axis=0)
    h = _rms(xx, g_ref[...]).astype(BF16)
    u = jnp.dot(h, win_ref[...], preferred_element_type=F32)
    keep = jnp.where(first, 0.0, 1.0)
    rowi = lax.broadcasted_iota(I32, (hal + tm, 1), 0)
    u_s[...] = u * jnp.where(rowi < hal, keep, 1.0)
    pos = (pl.program_id(0) % tiles_per_seq) * tm + lax.broadcasted_iota(I32, (tm, 1), 0)
    mixed = []
    for gi, w in enumerate(POOL_WINDOWS):
        cs = slice(gi * gd, (gi + 1) * gd)
        ssum = u_s[hal:hal + tm, cs]
        for dlt in range(1, w):
            ssum = ssum + u_s[hal - dlt:hal - dlt + tm, cs]
        cnt = jnp.minimum(pos + 1, w).astype(F32)
        pooled = ssum / cnt - u_s[hal:hal + tm, cs]
        mixed.append(jnp.dot(pooled.astype(BF16), wgrp_ref[gi], preferred_element_type=F32))
    o_ref[...] = x + jnp.concatenate(mixed, axis=1) * sc_ref[...]


def pool_mixer(x, gain, w_in, w_group, scale, seq, tm=ROW_TILE):
    t, d = x.shape
    hal = max(POOL_WINDOWS)
    g, gd, _ = w_group.shape
    return pl.pallas_call(
        functools.partial(_pool_kernel, tiles_per_seq=seq // tm),
        grid=(t // tm,),
        in_specs=[pl.BlockSpec((tm, d), lambda i: (i, 0)),
                  pl.BlockSpec((hal, d), lambda i: (jnp.maximum(i * (tm // hal) - 1, 0), 0)),
                  _const_spec((1, d)), _const_spec((d, d)), _const_spec((g, gd, gd)),
                  _const_spec((1, d))],
        out_specs=pl.BlockSpec((tm, d), lambda i: (i, 0)),
        out_shape=jax.ShapeDtypeStruct((t, d), F32),
        scratch_shapes=[pltpu.VMEM((hal + tm, d), F32)],
        compiler_params=_cparams(1), name="pool_mixer",
    )(x, x, gain.reshape(1, d), w_in, w_group, scale.reshape(1, d))


def _swiglu_acc(h, wg_ref, wu_ref, wd_ref, acc, widx):
    ff = wg_ref.shape[-1]
    for c0 in range(0, ff, FF_CHUNK):
        cs = slice(c0, c0 + FF_CHUNK)
        gate = jnp.dot(h, wg_ref[widx + (slice(None), cs)], preferred_element_type=F32)
        up = jnp.dot(h, wu_ref[widx + (slice(None), cs)], preferred_element_type=F32)
        a = (gate * _sigmoid(gate) * up).astype(BF16)
        acc = acc + jnp.dot(a, wd_ref[widx + (cs, slice(None))], preferred_element_type=F32)
    return acc


def _ffn_kernel(x_ref, a_ref, wo_ref, g_ref, wg_ref, wu_ref, wd_ref, o_ref):
    x = x_ref[...] + jnp.dot(a_ref[...], wo_ref[...], preferred_element_type=F32)
    h = _rms(x, g_ref[...]).astype(BF16)
    o_ref[...] = _swiglu_acc(h, wg_ref, wu_ref, wd_ref, x, ())


def _ffn_plain_kernel(x_ref, g_ref, wg_ref, wu_ref, wd_ref, o_ref):
    x = x_ref[...]
    h = _rms(x, g_ref[...]).astype(BF16)
    o_ref[...] = _swiglu_acc(h, wg_ref, wu_ref, wd_ref, x, ())


def ffn_dense(x, proj, gain, wg, wu, wd, tm=ROW_TILE):
    t, d = x.shape
    ff = wg.shape[1]
    row_spec = pl.BlockSpec((tm, d), lambda i: (i, 0))
    w_specs = [_const_spec((1, d)), _const_spec((d, ff)), _const_spec((d, ff)), _const_spec((ff, d))]
    if proj is None:
        body, in_specs, args = _ffn_plain_kernel, [row_spec], (x,)
    else:
        a, w_out = proj
        ka = a.shape[1]
        body, args = _ffn_kernel, (x, a, w_out)
        in_specs = [row_spec, pl.BlockSpec((tm, ka), lambda i: (i, 0)), _const_spec((ka, d))]
    return pl.pallas_call(
        body,
        grid=(t // tm,),
        in_specs=in_specs + w_specs,
        out_specs=row_spec,
        out_shape=jax.ShapeDtypeStruct((t, d), F32),
        compiler_params=_cparams(1), name="ffn_dense",
    )(*args, gain.reshape(1, d), wg, wu, wd)


def _moe_ffn_kernel(texp_ref, nused_ref, x_ref, g_ref, wg_ref, wu_ref, wd_ref, o_ref):
    live = pl.program_id(0) < nused_ref[0]

    @pl.when(live)
    def _():
        h = _rms(x_ref[...], g_ref[...]).astype(BF16)
        o_ref[...] = _swiglu_acc(h, wg_ref, wu_ref, wd_ref, jnp.zeros(o_ref.shape, F32), (0,))

    @pl.when(jnp.logical_not(live))
    def _():
        o_ref[...] = jnp.zeros_like(o_ref)


def moe_ffn(xs, gain, wg, wu, wd, tile_expert, n_used, tm):
    r, d = xs.shape
    ff = wg.shape[2]
    live = lambda i, te, nu: jnp.minimum(i, nu[0] - 1)
    grid_spec = pltpu.PrefetchScalarGridSpec(
        num_scalar_prefetch=2,
        grid=(r // tm,),
        in_specs=[pl.BlockSpec((tm, d), lambda i, te, nu: (live(i, te, nu), 0)),
                  pl.BlockSpec((1, d), lambda i, te, nu: (0, 0)),
                  pl.BlockSpec((1, d, ff), lambda i, te, nu: (te[i], 0, 0)),
                  pl.BlockSpec((1, d, ff), lambda i, te, nu: (te[i], 0, 0)),
                  pl.BlockSpec((1, ff, d), lambda i, te, nu: (te[i], 0, 0))],
        out_specs=pl.BlockSpec((tm, d), lambda i, te, nu: (i, 0)),
    )
    return pl.pallas_call(
        _moe_ffn_kernel, grid_spec=grid_spec,
        out_shape=jax.ShapeDtypeStruct((r, d), F32),
        compiler_params=_cparams(1), name="moe_ffn",
    )(tile_expert, n_used, xs, gain.reshape(1, d), wg, wu, wd)


def _router_kernel(x_ref, a_ref, wo_ref, g_ref, wr_ref, xo_ref, idx_ref, prob_ref, *, ne):
    x = x_ref[...] + jnp.dot(a_ref[...], wo_ref[...], preferred_element_type=F32)
    xo_ref[...] = x
    h = _rms(x, g_ref[...])
    h_hi = h.astype(BF16)
    h_lo = (h - h_hi.astype(F32)).astype(BF16)
    wr = wr_ref[...]
    w_hi = wr.astype(BF16)
    w_lo = (wr - w_hi.astype(F32)).astype(BF16)
    logits = (jnp.dot(h_hi, w_hi, preferred_element_type=F32)
              + jnp.dot(h_lo, w_hi, preferred_element_type=F32)
              + jnp.dot(h_hi, w_lo, preferred_element_type=F32))
    col = lax.broadcasted_iota(I32, logits.shape, 1).astype(F32)
    logits = jnp.where(col < ne, logits, -jnp.inf)
    m1 = jnp.max(logits, axis=1, keepdims=True)
    i1 = jnp.min(jnp.where(logits == m1, col, float(ne)), axis=1, keepdims=True)
    rest = jnp.where(col == i1, -jnp.inf, logits)
    m2 = jnp.max(rest, axis=1, keepdims=True)
    i2 = jnp.min(jnp.where(rest == m2, col, float(ne)), axis=1, keepdims=True)
    e2 = jnp.exp(m2 - m1)
    p1 = 1.0 / (1.0 + e2)
    idx_ref[...] = jnp.where(col == 0.0, i1, i2).astype(I32)
    prob_ref[...] = jnp.where(col == 0.0, p1, e2 * p1)


def router_top2(x, a, w_out, gain, wr, tm=ROW_TILE):
    t, d = x.shape
    ka = a.shape[1]
    ne = wr.shape[1]
    wr_pad = jnp.zeros((d, LANE), F32).at[:, :ne].set(wr)
    return pl.pallas_call(
        functools.partial(_router_kernel, ne=ne),
        grid=(t // tm,),
        in_specs=[pl.BlockSpec((tm, d), lambda i: (i, 0)), pl.BlockSpec((tm, ka), lambda i: (i, 0)),
                  _const_spec((ka, d)), _const_spec((1, d)), _const_spec((d, LANE))],
        out_specs=[pl.BlockSpec((tm, d), lambda i: (i, 0)),
                   pl.BlockSpec((tm, LANE), lambda i: (i, 0)), pl.BlockSpec((tm, LANE), lambda i: (i, 0))],
        out_shape=[jax.ShapeDtypeStruct((t, d), F32),
                   jax.ShapeDtypeStruct((t, LANE), I32), jax.ShapeDtypeStruct((t, LANE), F32)],
        compiler_params=_cparams(1), name="router_top2",
    )(x, a, w_out, gain.reshape(1, d), wr_pad)


def _dispatch_kernel(pos_ref, x_ref, xs_in, xs_hbm, sem):
    del xs_in
    tm = x_ref.shape[0]

    def issue(k, c):
        pltpu.make_async_copy(x_ref.at[pl.ds(k, 1)], xs_hbm.at[pl.ds(pos_ref[0, 0, k], 1)], sem).start()
        pltpu.make_async_copy(x_ref.at[pl.ds(k, 1)], xs_hbm.at[pl.ds(pos_ref[0, 1, k], 1)], sem).start()
        return c

    lax.fori_loop(0, tm, issue, 0, unroll=8)
    pltpu.make_async_copy(x_ref, xs_hbm.at[pl.ds(0, tm)], sem).wait()
    pltpu.make_async_copy(x_ref, xs_hbm.at[pl.ds(0, tm)], sem).wait()


def dispatch_rows(x, pos, seed, tm=DISPATCH_ROWS):
    t, d = x.shape
    n_rows = seed.shape[0]
    pos3 = pos.reshape(t // tm, tm, 2).transpose(0, 2, 1)
    return pl.pallas_call(
        _dispatch_kernel,
        grid=(t // tm,),
        in_specs=[pl.BlockSpec((1, 2, tm), lambda i: (i, 0, 0), memory_space=pltpu.SMEM),
                  pl.BlockSpec((tm, d), lambda i: (i, 0)),
                  pl.BlockSpec(memory_space=pl.ANY)],
        out_specs=pl.BlockSpec(memory_space=pl.ANY),
        out_shape=jax.ShapeDtypeStruct((n_rows, d), x.dtype),
        input_output_aliases={2: 0},
        scratch_shapes=[pltpu.SemaphoreType.DMA(())],
        compiler_params=_cparams(1), name="dispatch_rows",
    )(pos3, x, seed)


def _combine_kernel(pos_ref, x_ref, p_ref, g_ref, y_hbm, o_ref, buf, sem, *, final_norm):
    tm = x_ref.shape[0]

    def issue(k, c):
        pltpu.make_async_copy(y_hbm.at[pl.ds(pos_ref[0, 0, k], 1)], buf.at[0, pl.ds(k, 1)], sem).start()
        pltpu.make_async_copy(y_hbm.at[pl.ds(pos_ref[0, 1, k], 1)], buf.at[1, pl.ds(k, 1)], sem).start()
        return c

    lax.fori_loop(0, tm, issue, 0, unroll=8)
    pltpu.make_async_copy(y_hbm.at[pl.ds(0, tm)], buf.at[0], sem).wait()
    pltpu.make_async_copy(y_hbm.at[pl.ds(0, tm)], buf.at[1], sem).wait()
    p = p_ref[...]
    out = x_ref[...] + p[:, 0:1] * buf[0] + p[:, 1:2] * buf[1]
    if final_norm:
        out = _rms(out, g_ref[...])
    o_ref[...] = out


def moe_combine(x, ys, pos, prob, final_gain, tm=256):
    t, d = x.shape
    final_norm = final_gain is not None
    gain = final_gain if final_norm else jnp.ones((d,), F32)
    pos3 = pos.reshape(t // tm, tm, 2).transpose(0, 2, 1)
    return pl.pallas_call(
        functools.partial(_combine_kernel, final_norm=final_norm),
        grid=(t // tm,),
        in_specs=[pl.BlockSpec((1, 2, tm), lambda i: (i, 0, 0), memory_space=pltpu.SMEM),
                  pl.BlockSpec((tm, d), lambda i: (i, 0)),
                  pl.BlockSpec((tm, LANE), lambda i: (i, 0)),
                  _const_spec((1, d)),
                  pl.BlockSpec(memory_space=pl.ANY)],
        out_specs=pl.BlockSpec((tm, d), lambda i: (i, 0)),
        out_shape=jax.ShapeDtypeStruct((t, d), F32),
        scratch_shapes=[pltpu.VMEM((2, tm, d), F32), pltpu.SemaphoreType.DMA(())],
        compiler_params=_cparams(1), name="moe_combine",
    )(pos3, x, prob, gain.reshape(1, d), ys)


def _final_norm_kernel(x_ref, g_ref, o_ref):
    o_ref[...] = _rms(x_ref[...], g_ref[...])


def final_norm(x, gain, tm=ROW_TILE):
    t, d = x.shape
    return pl.pallas_call(
        _final_norm_kernel, grid=(t // tm,),
        in_specs=[pl.BlockSpec((tm, d), lambda i: (i, 0)), _const_spec((1, d))],
        out_specs=pl.BlockSpec((tm, d), lambda i: (i, 0)),
        out_shape=jax.ShapeDtypeStruct((t, d), F32),
        compiler_params=_cparams(1), name="final_norm",
    )(x, gain.reshape(1, d))


def _route_plan(idx, tm):
    t = idx.shape[0]
    e_flat = idx.reshape(-1)
    onehot = (e_flat[:, None] == jnp.arange(N_EXPERTS, dtype=I32)[None, :]).astype(I32)
    csum = jnp.cumsum(onehot, axis=0)
    rank = jnp.take_along_axis(csum, e_flat[:, None], axis=1)[:, 0] - 1
    counts = csum[-1]
    padded = ((counts + tm - 1) // tm) * tm
    ends = jnp.cumsum(padded)
    pos = (ends - padded)[e_flat] + rank
    n_rows = 2 * t + N_EXPERTS * tm
    n_used = (ends[-1] // tm).astype(I32)
    tile_start = jnp.arange(n_rows // tm, dtype=I32) * tm
    tile_start = jnp.minimum(tile_start, ends[-1] - 1)
    tile_expert = jnp.sum((ends[None, :] <= tile_start[:, None]).astype(I32), axis=1)
    tile_expert = jnp.minimum(tile_expert, N_EXPERTS - 1)
    return pos.reshape(t, 2), tile_expert, n_used.reshape(1), n_rows


def moe_layer(x, proj, gain, router, wg, wu, wd, final_gain, seed=None, tm=MOE_TILE, tc=DISPATCH_ROWS):
    x, idx, prob = router_top2(x, proj[0], proj[1], gain, router)
    pos, tile_expert, n_used, n_rows = _route_plan(idx[:, :2], tm)
    if seed is None:
        seed = jnp.zeros((n_rows, x.shape[1]), x.dtype)
    xs = dispatch_rows(x, pos, seed, tm=tc)
    ys = moe_ffn(xs, gain, wg, wu, wd, tile_expert, n_used, tm)
    return moe_combine(x, ys, pos, prob, final_gain, tm=tc), xs


def gla_layer(x, gain, w_in, w_gate_up, b_gate, gnorm, w_out, batch, seq):
    d = x.shape[1]
    n_main = w_in.shape[1] - GLA_GATE_RANK
    w_all = jnp.zeros((d, n_main + LANE), BF16).at[:, :w_in.shape[1]].set(w_in.astype(BF16))
    w_gu = jnp.zeros((LANE, w_gate_up.shape[1]), F32).at[:GLA_GATE_RANK].set(w_gate_up)
    qkvg, la = gla_proj(x, gain, w_all, w_gu, b_gate)
    og = gla_core(qkvg, la, gnorm, batch, seq, rows=min(GLA_ROWS, seq))
    return og, w_out.astype(BF16)


def sba_layer(x, gain, w_in, w_out, batch, seq):
    qkv = norm_proj(x, gain, w_in.astype(BF16))
    return sba_core(qkv, batch, seq), w_out.astype(BF16)


def kernel(x, norm_mix, norm_ffn, norm_final, gla_w_in, gla_w_gate_up, gla_b_gate, gla_norm, gla_w_out, sba_w_in, sba_w_out, pool_w_in, pool_w_group, pool_scale, ffn_w_gate, ffn_w_up, ffn_w_down, moe_router, moe_w_gate, moe_w_up, moe_w_down):
    batch, seq, d = x.shape
    depth = norm_mix.shape[0]
    x = x.reshape(batch * seq, d)
    normed = False
    sorted_rows = None
    for i in range(depth):
        kind, j = i % 3, i // 3
        if kind == 0:
            proj = gla_layer(x, norm_mix[i], gla_w_in[j], gla_w_gate_up[j], gla_b_gate[j],
                             gla_norm[j], gla_w_out[j], batch, seq)
        elif kind == 1:
            proj = sba_layer(x, norm_mix[i], sba_w_in[j], sba_w_out[j], batch, seq)
        else:
            proj = None
            x = pool_mixer(x, norm_mix[i], pool_w_in[j].astype(BF16), pool_w_group[j].astype(BF16),
                           pool_scale[j], seq)
        f = i // 2
        if i % 2 == 0:
            x = ffn_dense(x, proj, norm_ffn[i], ffn_w_gate[f].astype(BF16), ffn_w_up[f].astype(BF16),
                          ffn_w_down[f].astype(BF16))
        else:
            if proj is None:
                raise NotImplementedError("pooling mixer followed by the routed block")
            last = i == depth - 1
            x, sorted_rows = moe_layer(x, proj, norm_ffn[i], moe_router[f], moe_w_gate[f].astype(BF16),
                                       moe_w_up[f].astype(BF16), moe_w_down[f].astype(BF16),
                                       norm_final if last else None, seed=sorted_rows)
            normed = last
    if not normed:
        x = final_norm(x, norm_final)
    return x.reshape(batch, seq, d)
```

```python
import functools

import numpy as np
import jax
import jax.numpy as jnp
from jax import lax
from jax.experimental import pallas as pl
from jax.experimental.pallas import tpu as pltpu

F32 = jnp.float32
BF16 = jnp.bfloat16
I32 = jnp.int32
HIGHEST = lax.Precision.HIGHEST

RMS_EPS = 1e-6
GLA_HEADS = 4
GLA_GATE_RANK = 16
GLA_GATE_NORMALIZER = 16.0
SBA_HEADS = 16
POOL_WINDOWS = (2, 4, 8, 16)
N_EXPERTS = 8

LANE = 128
VMEM_LIMIT = 56 * 1024 * 1024
ROW_TILE = 512
FF_CHUNK = 256
GLA_CHUNK = 64
GLA_SUB = 16
GLA_ROWS = 512
GLA_SAFE_EXP = 60.0
SBA_BQ = 256
SBA_ROWS = 2048
SBA_DEAD2 = 150.0
LOG2E = 1.4426950408889634
MOE_TILE = 512
DISPATCH_ROWS = 512


def _cparams(n_axes):
    return pltpu.CompilerParams(dimension_semantics=("arbitrary",) * n_axes,
                                vmem_limit_bytes=VMEM_LIMIT)


def _rms(xf, gain):
    return xf * lax.rsqrt(jnp.mean(xf * xf, axis=-1, keepdims=True) + RMS_EPS) * gain


def _sigmoid(x):
    return 1.0 / (1.0 + jnp.exp(-x))


def _log_sigmoid(z):
    return jnp.minimum(z, 0.0) - jnp.log(1.0 + jnp.exp(-jnp.abs(z)))


def _dot_split(a, b, split_lhs=False):
    f = a if split_lhs else b
    hi = f.astype(BF16)
    lo = (f - hi.astype(F32)).astype(BF16)
    if split_lhs:
        return (jnp.dot(hi, b, preferred_element_type=F32) + jnp.dot(lo, b, preferred_element_type=F32))
    return (jnp.dot(a, hi, preferred_element_type=F32) + jnp.dot(a, lo, preferred_element_type=F32))


def _const_spec(shape):
    return pl.BlockSpec(shape, lambda *_: (0,) * len(shape))


def _norm_proj_kernel(x_ref, g_ref, w_ref, o_ref, *, nchunk):
    h = _rms(x_ref[...], g_ref[...]).astype(BF16)
    n = o_ref.shape[1]
    for c0 in range(0, n, nchunk):
        o_ref[:, c0:c0 + nchunk] = jnp.dot(
            h, w_ref[:, c0:c0 + nchunk], preferred_element_type=F32).astype(o_ref.dtype)


def norm_proj(x, gain, w, tm=ROW_TILE):
    t, d = x.shape
    n = w.shape[1]
    return pl.pallas_call(
        functools.partial(_norm_proj_kernel, nchunk=512),
        grid=(t // tm,),
        in_specs=[pl.BlockSpec((tm, d), lambda i: (i, 0)),
                  _const_spec((1, d)), _const_spec((d, n))],
        out_specs=pl.BlockSpec((tm, n), lambda i: (i, 0)),
        out_shape=jax.ShapeDtypeStruct((t, n), BF16),
        compiler_params=_cparams(1), name="norm_proj",
    )(x, gain.reshape(1, d), w)


def _gla_proj_kernel(x_ref, g_ref, w_ref, wgu_ref, bg_ref, o_ref, la_ref, *, nchunk):
    h = _rms(x_ref[...], g_ref[...]).astype(BF16)
    n = o_ref.shape[1]
    for c0 in range(0, n, nchunk):
        o_ref[:, c0:c0 + nchunk] = jnp.dot(
            h, w_ref[:, c0:c0 + nchunk], preferred_element_type=F32).astype(o_ref.dtype)
    a = jnp.dot(h, w_ref[:, n:], preferred_element_type=F32)
    z = jnp.dot(a, wgu_ref[...], precision=HIGHEST, preferred_element_type=F32) + bg_ref[...]
    la_ref[...] = _log_sigmoid(z) / GLA_GATE_NORMALIZER


def gla_proj(x, gain, w_all, w_gu, b_gate, tm=ROW_TILE):
    t, d = x.shape
    n = w_all.shape[1] - LANE
    dk = w_gu.shape[1]
    return pl.pallas_call(
        functools.partial(_gla_proj_kernel, nchunk=512),
        grid=(t // tm,),
        in_specs=[pl.BlockSpec((tm, d), lambda i: (i, 0)),
                  _const_spec((1, d)), _const_spec((d, n + LANE)),
                  _const_spec((LANE, dk)), _const_spec((1, dk))],
        out_specs=[pl.BlockSpec((tm, n), lambda i: (i, 0)),
                   pl.BlockSpec((tm, dk), lambda i: (i, 0))],
        out_shape=[jax.ShapeDtypeStruct((t, n), BF16), jax.ShapeDtypeStruct((t, dk), F32)],
        compiler_params=_cparams(1), name="gla_proj",
    )(x, gain.reshape(1, d), w_all, w_gu, b_gate.reshape(1, dk))


def _gla_tables():
    c, s = GLA_CHUNK, GLA_SUB
    i = np.arange(c)[:, None]
    j = np.arange(c)[None, :]
    lo = (i // s) * s
    hi = lo + s
    a1 = (j >= lo) & (j <= i)
    a2 = (j > i) & (j < hi)
    a3 = j < lo
    a4 = j >= hi
    amat = np.concatenate([a1, a2, a3, a4], axis=0).astype(np.float32)
    row = np.arange(s * LANE)[:, None] // LANE
    erep = (row == (np.arange(c)[None, :] % s)).astype(np.float32)
    return jnp.asarray(amat, dtype=BF16), jnp.asarray(erep, dtype=BF16)


def _gla_kernel(q_ref, k_ref, v_ref, g_ref, la_ref, gn_ref, amat_ref, erep_ref,
                og_ref, state_s, kf_s, dq_s, tm_s, *, qscale):
    c, s = GLA_CHUNK, GLA_SUB
    nsub = c // s
    nh = state_s.shape[0]
    dk = q_ref.shape[1] // nh
    dv = v_ref.shape[1] // nh

    @pl.when(pl.program_id(1) == 0)
    def _():
        state_s[...] = jnp.zeros_like(state_s)

    il = lax.broadcasted_iota(I32, (s, dk), 0)
    ii = lax.broadcasted_iota(I32, (c, c), 0)
    jj = lax.broadcasted_iota(I32, (c, c), 1)
    ri, rj = ii // s, jj // s
    on_diag = jnp.logical_and(ri == rj, ii >= jj)
    rowblk = lax.broadcasted_iota(I32, (c, dk), 0) // s
    ones_ck = jnp.ones((c, dk), BF16)

    def head_chunk(h, r0, exact_diag):
        kcols = slice(h * dk, (h + 1) * dk)
        vcols = slice(h * dv, (h + 1) * dv)
        la = la_ref[pl.ds(r0, c), kcols]
        qf = q_ref[pl.ds(r0, c), kcols].astype(F32) * qscale
        kf = k_ref[pl.ds(r0, c), kcols].astype(F32)
        vb = v_ref[pl.ds(r0, c), vcols]
        cum = _dot_split(amat_ref[...], la)
        dq, dkk, cprev, ctail = cum[0:c], cum[c:2 * c], cum[2 * c:3 * c], cum[3 * c:4 * c]
        cnext = cprev + dq + dkk
        qt = qf * jnp.exp(dq)
        kt = kf * jnp.exp(dkk)
        if exact_diag:
            kf_s[h] = kf
            dq_s[h] = dq
            for r in range(nsub):
                qb = qf[r * s:(r + 1) * s]
                dqb = dq[r * s:(r + 1) * s]
                for jl in range(s):
                    krow = kf_s[h, pl.ds(r * s + jl, 1), :]
                    drow = dq_s[h, pl.ds(r * s + jl, 1), :]
                    dec = jnp.where(il >= jl, jnp.exp(jnp.minimum(dqb - drow, 0.0)), 0.0)
                    tm_s[h, r * s:(r + 1) * s, jl * dk:(jl + 1) * dk] = (qb * krow * dec).astype(BF16)
            sdiag = jnp.dot(tm_s[h], erep_ref[...], preferred_element_type=F32)
        else:
            kd = (kf * jnp.exp(-dq)).astype(BF16)
            sdiag = lax.dot_general(qt.astype(BF16), kd, (((1,), (1,)), ((), ())),
                                    preferred_element_type=F32)
        soff = jnp.zeros((c, c), F32)
        for rp in range(nsub - 1):
            cn = cnext[rp * s:rp * s + 1, :]
            p = (qt * jnp.exp(jnp.minimum(cprev - cn, 0.0))).astype(BF16)
            kp = jnp.where(rowblk == rp, kt, 0.0).astype(BF16)
            soff = soff + lax.dot_general(p, kp, (((1,), (1,)), ((), ())),
                                          preferred_element_type=F32)
        sc = jnp.where(on_diag, sdiag, jnp.where(ri > rj, soff, 0.0))
        st = state_s[h]
        o = jnp.dot(sc.astype(BF16), vb, preferred_element_type=F32) + jnp.dot(
            (qt * jnp.exp(cprev)).astype(BF16), st.astype(BF16), preferred_element_type=F32)
        ks_t = (kt * jnp.exp(ctail)).T.astype(BF16)
        upd = jnp.dot(ks_t, vb, preferred_element_type=F32)
        dcol = jnp.exp(_dot_split(la.T, ones_ck, split_lhs=True))
        state_s[h] = jnp.concatenate([dcol] * (dv // dk), axis=1) * st + upd
        gf = g_ref[pl.ds(r0, c), vcols].astype(F32)
        og_ref[pl.ds(r0, c), vcols] = (_rms(o, gn_ref[...]) * (gf * _sigmoid(gf))).astype(og_ref.dtype)

    def sweep(exact_diag):
        def chunk(ci, carry):
            r0 = pl.multiple_of(ci * c, c)
            for h in range(nh):
                head_chunk(h, r0, exact_diag)
            return carry

        lax.fori_loop(0, q_ref.shape[0] // c, chunk, 0, unroll=1 if exact_diag else 2)

    safe = jnp.max(-la_ref[...]) * s < GLA_SAFE_EXP
    pl.when(safe)(lambda: sweep(False))
    pl.when(jnp.logical_not(safe))(lambda: sweep(True))


def gla_core(qkvg, la, gnorm, batch, seq, rows=GLA_ROWS):
    t = qkvg.shape[0]
    dkt = la.shape[1]
    dk = dkt // GLA_HEADS
    dvt = (qkvg.shape[1] - 2 * dkt) // 2
    dv = dvt // GLA_HEADS
    nblk = seq // rows
    amat, erep = _gla_tables()
    row = lambda b, c: b * nblk + c
    return pl.pallas_call(
        functools.partial(_gla_kernel, qscale=float(dk) ** -0.5),
        grid=(batch, nblk),
        in_specs=[pl.BlockSpec((rows, dkt), lambda b, c: (row(b, c), 0)),
                  pl.BlockSpec((rows, dkt), lambda b, c: (row(b, c), 1)),
                  pl.BlockSpec((rows, dvt), lambda b, c: (row(b, c), 2 * dkt // dvt)),
                  pl.BlockSpec((rows, dvt), lambda b, c: (row(b, c), 2 * dkt // dvt + 1)),
                  pl.BlockSpec((rows, dkt), lambda b, c: (row(b, c), 0)),
                  _const_spec((1, dv)), _const_spec(amat.shape), _const_spec(erep.shape)],
        out_specs=pl.BlockSpec((rows, dvt), lambda b, c: (row(b, c), 0)),
        out_shape=jax.ShapeDtypeStruct((t, dvt), BF16),
        scratch_shapes=[pltpu.VMEM((GLA_HEADS, dk, dv), F32),
                        pltpu.VMEM((GLA_HEADS, GLA_CHUNK, dk), F32),
                        pltpu.VMEM((GLA_HEADS, GLA_CHUNK, dk), F32),
                        pltpu.VMEM((GLA_HEADS, GLA_CHUNK, GLA_SUB * dk), BF16)],
        compiler_params=_cparams(2), name="gla_core",
    )(qkvg, qkvg, qkvg, qkvg, la, gnorm.reshape(1, dv), amat, erep)


def _sba_kernel(q_ref, k_ref, v_ref, u_ref, o_ref, qs_s, acc_s, carry_s, *, scale, dh):
    bq = SBA_BQ
    nq = q_ref.shape[0] // bq
    step = pl.program_id(2)
    lane = lax.broadcasted_iota(I32, (bq, LANE), 1)
    row = lax.broadcasted_iota(I32, (2 * bq, bq), 0)
    strict = lax.broadcasted_iota(I32, (2 * bq, bq), 1) < jnp.where(row >= bq, row - bq, row)

    def block(jb, diagonal):
        k0 = pl.multiple_of(jb * bq, bq)
        kb = k_ref[pl.ds(k0, bq), :]
        vb = v_ref[pl.ds(k0, bq), :]
        z = lax.dot_general(qs_s[...], kb, (((1,), (1,)), ((), ())), preferred_element_type=F32)
        sp = jnp.maximum(z, 0.0) + jnp.log2(1.0 + jnp.exp2(jnp.minimum(z, -z)))
        if diagonal:
            sp = jnp.where(strict, sp, 0.0)
        ct = jnp.dot(sp.astype(BF16), u_ref[...], preferred_element_type=F32)
        carry = carry_s[...]
        w = jnp.exp2((z - sp) - ct[:, :bq] - jnp.concatenate([carry] * (bq // LANE), axis=1))
        if diagonal:
            w = jnp.where(strict, w, 0.0)
        acc_s[...] += jnp.dot(w.astype(BF16), vb, preferred_element_type=F32)
        carry = carry + ct[:, bq:]
        carry_s[...] = carry
        return jnp.min(carry)

    def qblock(qi, c):
        r0 = pl.multiple_of(qi * bq, bq)
        q2 = q_ref[pl.ds(r0, bq), :].astype(F32) * scale
        qs_s[0:bq] = jnp.where(lane < dh, q2, 0.0).astype(BF16)
        qs_s[bq:2 * bq] = jnp.where(lane >= dh, q2, 0.0).astype(BF16)
        acc_s[...] = jnp.zeros_like(acc_s)
        carry_s[...] = jnp.zeros_like(carry_s)
        jd = step * nq + qi
        low = block(jd, True)
        lax.while_loop(lambda st: jnp.logical_and(st[0] >= 0, st[1] < SBA_DEAD2),
                       lambda st: (st[0] - 1, block(st[0], False)), (jd - 1, low))
        a = acc_s[...]
        o_ref[pl.ds(r0, bq), :] = jnp.where(lane < dh, a[:bq], a[bq:]).astype(o_ref.dtype)
        return c

    lax.fori_loop(0, nq, qblock, 0)


def sba_core(qkv, batch, seq):
    t = qkv.shape[0]
    d = qkv.shape[1] // 3
    dh = d // SBA_HEADS
    npair = d // LANE
    bq = SBA_BQ
    rows = min(SBA_ROWS, seq)
    nstep = seq // rows
    u = np.concatenate([np.tril(np.ones((bq, bq), np.float32), -1), np.ones((bq, LANE), np.float32)], 1)
    return pl.pallas_call(
        functools.partial(_sba_kernel, scale=float(dh) ** -0.5 * LOG2E, dh=dh),
        grid=(batch, npair, nstep),
        in_specs=[pl.BlockSpec((rows, LANE), lambda b, p, i: (b * nstep + i, p)),
                  pl.BlockSpec((seq, LANE), lambda b, p, i: (b, npair + p)),
                  pl.BlockSpec((seq, LANE), lambda b, p, i: (b, 2 * npair + p)),
                  _const_spec((bq, bq + LANE))],
        out_specs=pl.BlockSpec((rows, LANE), lambda b, p, i: (b * nstep + i, p)),
        out_shape=jax.ShapeDtypeStruct((t, d), BF16),
        scratch_shapes=[pltpu.VMEM((2 * bq, LANE), BF16), pltpu.VMEM((2 * bq, LANE), F32),
                        pltpu.VMEM((2 * bq, LANE), F32)],
        compiler_params=_cparams(3), name="sba_core",
    )(qkv, qkv, qkv, jnp.asarray(u, dtype=BF16))


def _pool_kernel(x_ref, halo_ref, g_ref, win_ref, wgrp_ref, sc_ref, o_ref, u_s, *, tiles_per_seq):
    tm, d = x_ref.shape
    hal = halo_ref.shape[0]
    gd = wgrp_ref.shape[1]
    first = (pl.program_id(0) % tiles_per_seq) == 0
    x = x_ref[...]
    xx = jnp.concatenate([halo_ref[...], x], axis=0)
    h = _rms(xx, g_ref[...]).astype(BF16)
    u = jnp.dot(h, win_ref[...], preferred_element_type=F32)
    keep = jnp.where(first, 0.0, 1.0)
    rowi = lax.broadcasted_iota(I32, (hal + tm, 1), 0)
    u_s[...] = u * jnp.where(rowi < hal, keep, 1.0)
    pos = (pl.program_id(0) % tiles_per_seq) * tm + lax.broadcasted_iota(I32, (tm, 1), 0)
    mixed = []
    for gi, w in enumerate(POOL_WINDOWS):
        cs = slice(gi * gd, (gi + 1) * gd)
        ssum = u_s[hal:hal + tm, cs]
        for dlt in range(1, w):
            ssum = ssum + u_s[hal - dlt:hal - dlt + tm, cs]
        cnt = jnp.minimum(pos + 1, w).astype(F32)
        pooled = ssum / cnt - u_s[hal:hal + tm, cs]
        mixed.append(jnp.dot(pooled.astype(BF16), wgrp_ref[gi], preferred_element_type=F32))
    o_ref[...] = x + jnp.concatenate(mixed, axis=1) * sc_ref[...]


def pool_mixer(x, gain, w_in, w_group, scale, seq, tm=ROW_TILE):
    t, d = x.shape
    hal = max(POOL_WINDOWS)
    g, gd, _ = w_group.shape
    return pl.pallas_call(
        functools.partial(_pool_kernel, tiles_per_seq=seq // tm),
        grid=(t // tm,),
        in_specs=[pl.BlockSpec((tm, d), lambda i: (i, 0)),
                  pl.BlockSpec((hal, d), lambda i: (jnp.maximum(i * (tm // hal) - 1, 0), 0)),
                  _const_spec((1, d)), _const_spec((d, d)), _const_spec((g, gd, gd)),
                  _const_spec((1, d))],
        out_specs=pl.BlockSpec((tm, d), lambda i: (i, 0)),
        out_shape=jax.ShapeDtypeStruct((t, d), F32),
        scratch_shapes=[pltpu.VMEM((hal + tm, d), F32)],
        compiler_params=_cparams(1), name="pool_mixer",
    )(x, x, gain.reshape(1, d), w_in, w_group, scale.reshape(1, d))


def _swiglu_acc(h, wg_ref, wu_ref, wd_ref, acc, widx):
    ff = wg_ref.shape[-1]
    for c0 in range(0, ff, FF_CHUNK):
        cs = slice(c0, c0 + FF_CHUNK)
        gate = jnp.dot(h, wg_ref[widx + (slice(None), cs)], preferred_element_type=F32)
        up = jnp.dot(h, wu_ref[widx + (slice(None), cs)], preferred_element_type=F32)
        a = (gate * _sigmoid(gate) * up).astype(BF16)
        acc = acc + jnp.dot(a, wd_ref[widx + (cs, slice(None))], preferred_element_type=F32)
    return acc


def _ffn_kernel(x_ref, a_ref, wo_ref, g_ref, wg_ref, wu_ref, wd_ref, o_ref):
    x = x_ref[...] + jnp.dot(a_ref[...], wo_ref[...], preferred_element_type=F32)
    h = _rms(x, g_ref[...]).astype(BF16)
    o_ref[...] = _swiglu_acc(h, wg_ref, wu_ref, wd_ref, x, ())


def _ffn_plain_kernel(x_ref, g_ref, wg_ref, wu_ref, wd_ref, o_ref):
    x = x_ref[...]
    h = _rms(x, g_ref[...]).astype(BF16)
    o_ref[...] = _swiglu_acc(h, wg_ref, wu_ref, wd_ref, x, ())


def ffn_dense(x, proj, gain, wg, wu, wd, tm=ROW_TILE):
    t, d = x.shape
    ff = wg.shape[1]
    row_spec = pl.BlockSpec((tm, d), lambda i: (i, 0))
    w_specs = [_const_spec((1, d)), _const_spec((d, ff)), _const_spec((d, ff)), _const_spec((ff, d))]
    if proj is None:
        body, in_specs, args = _ffn_plain_kernel, [row_spec], (x,)
    else:
        a, w_out = proj
        ka = a.shape[1]
        body, args = _ffn_kernel, (x, a, w_out)
        in_specs = [row_spec, pl.BlockSpec((tm, ka), lambda i: (i, 0)), _const_spec((ka, d))]
    return pl.pallas_call(
        body,
        grid=(t // tm,),
        in_specs=in_specs + w_specs,
        out_specs=row_spec,
        out_shape=jax.ShapeDtypeStruct((t, d), F32),
        compiler_params=_cparams(1), name="ffn_dense",
    )(*args, gain.reshape(1, d), wg, wu, wd)


def _moe_ffn_kernel(texp_ref, nused_ref, x_ref, g_ref, wg_ref, wu_ref, wd_ref, o_ref):
    live = pl.program_id(0) < nused_ref[0]

    @pl.when(live)
    def _():
        h = _rms(x_ref[...], g_ref[...]).astype(BF16)
        o_ref[...] = _swiglu_acc(h, wg_ref, wu_ref, wd_ref, jnp.zeros(o_ref.shape, F32), (0, 0))

    @pl.when(jnp.logical_not(live))
    def _():
        o_ref[...] = jnp.zeros_like(o_ref)


def moe_ffn(xs, gain, wg, wu, wd, layer, tile_expert, n_used, tm):
    r, d = xs.shape
    ff = wg.shape[3]
    live = lambda i, te, nu: jnp.minimum(i, nu[0] - 1)
    grid_spec = pltpu.PrefetchScalarGridSpec(
        num_scalar_prefetch=2,
        grid=(r // tm,),
        in_specs=[pl.BlockSpec((tm, d), lambda i, te, nu: (live(i, te, nu), 0)),
                  pl.BlockSpec((1, d), lambda i, te, nu: (0, 0)),
                  pl.BlockSpec((1, 1, d, ff), lambda i, te, nu: (layer, te[i], 0, 0)),
                  pl.BlockSpec((1, 1, d, ff), lambda i, te, nu: (layer, te[i], 0, 0)),
                  pl.BlockSpec((1, 1, ff, d), lambda i, te, nu: (layer, te[i], 0, 0))],
        out_specs=pl.BlockSpec((tm, d), lambda i, te, nu: (i, 0)),
    )
    return pl.pallas_call(
        _moe_ffn_kernel, grid_spec=grid_spec,
        out_shape=jax.ShapeDtypeStruct((r, d), F32),
        compiler_params=_cparams(1), name="moe_ffn",
    )(tile_expert, n_used, xs, gain.reshape(1, d), wg, wu, wd)


def _router_kernel(x_ref, a_ref, wo_ref, g_ref, wr_ref, xo_ref, idx_ref, prob_ref, *, ne):
    x = x_ref[...] + jnp.dot(a_ref[...], wo_ref[...], preferred_element_type=F32)
    xo_ref[...] = x
    h = _rms(x, g_ref[...])
    h_hi = h.astype(BF16)
    h_lo = (h - h_hi.astype(F32)).astype(BF16)
    wr = wr_ref[...]
    w_hi = wr.astype(BF16)
    w_lo = (wr - w_hi.astype(F32)).astype(BF16)
    logits = (jnp.dot(h_hi, w_hi, preferred_element_type=F32)
              + jnp.dot(h_lo, w_hi, preferred_element_type=F32)
              + jnp.dot(h_hi, w_lo, preferred_element_type=F32))
    col = lax.broadcasted_iota(I32, logits.shape, 1).astype(F32)
    logits = jnp.where(col < ne, logits, -jnp.inf)
    m1 = jnp.max(logits, axis=1, keepdims=True)
    i1 = jnp.min(jnp.where(logits == m1, col, float(ne)), axis=1, keepdims=True)
    rest = jnp.where(col == i1, -jnp.inf, logits)
    m2 = jnp.max(rest, axis=1, keepdims=True)
    i2 = jnp.min(jnp.where(rest == m2, col, float(ne)), axis=1, keepdims=True)
    e2 = jnp.exp(m2 - m1)
    p1 = 1.0 / (1.0 + e2)
    idx_ref[...] = jnp.where(col == 0.0, i1, i2).astype(I32)
    prob_ref[...] = jnp.where(col == 0.0, p1, e2 * p1)


def router_top2(x, a, w_out, gain, wr, tm=ROW_TILE):
    t, d = x.shape
    ka = a.shape[1]
    ne = wr.shape[1]
    wr_pad = jnp.zeros((d, LANE), F32).at[:, :ne].set(wr)
    return pl.pallas_call(
        functools.partial(_router_kernel, ne=ne),
        grid=(t // tm,),
        in_specs=[pl.BlockSpec((tm, d), lambda i: (i, 0)), pl.BlockSpec((tm, ka), lambda i: (i, 0)),
                  _const_spec((ka, d)), _const_spec((1, d)), _const_spec((d, LANE))],
        out_specs=[pl.BlockSpec((tm, d), lambda i: (i, 0)),
                   pl.BlockSpec((tm, LANE), lambda i: (i, 0)), pl.BlockSpec((tm, LANE), lambda i: (i, 0))],
        out_shape=[jax.ShapeDtypeStruct((t, d), F32),
                   jax.ShapeDtypeStruct((t, LANE), I32), jax.ShapeDtypeStruct((t, LANE), F32)],
        compiler_params=_cparams(1), name="router_top2",
    )(x, a, w_out, gain.reshape(1, d), wr_pad)


def _dispatch_kernel(pos_ref, x_ref, xs_in, xs_hbm, sem):
    del xs_in
    tm = x_ref.shape[0]

    def issue(k, c):
        pltpu.make_async_copy(x_ref.at[pl.ds(k, 1)], xs_hbm.at[pl.ds(pos_ref[0, 0, k], 1)], sem).start()
        pltpu.make_async_copy(x_ref.at[pl.ds(k, 1)], xs_hbm.at[pl.ds(pos_ref[0, 1, k], 1)], sem).start()
        return c

    lax.fori_loop(0, tm, issue, 0, unroll=8)
    pltpu.make_async_copy(x_ref, xs_hbm.at[pl.ds(0, tm)], sem).wait()
    pltpu.make_async_copy(x_ref, xs_hbm.at[pl.ds(0, tm)], sem).wait()


def dispatch_rows(x, pos3, seed):
    t, d = x.shape
    n_rows = seed.shape[0]
    tm = pos3.shape[2]
    return pl.pallas_call(
        _dispatch_kernel,
        grid=(t // tm,),
        in_specs=[pl.BlockSpec((1, 2, tm), lambda i: (i, 0, 0), memory_space=pltpu.SMEM),
                  pl.BlockSpec((tm, d), lambda i: (i, 0)),
                  pl.BlockSpec(memory_space=pl.ANY)],
        out_specs=pl.BlockSpec(memory_space=pl.ANY),
        out_shape=jax.ShapeDtypeStruct((n_rows, d), x.dtype),
        input_output_aliases={2: 0},
        scratch_shapes=[pltpu.SemaphoreType.DMA(())],
        compiler_params=_cparams(1), name="dispatch_rows",
    )(pos3, x, seed)


def _combine_kernel(pos_ref, x_ref, p_ref, g_ref, y_hbm, o_ref, buf, sem, *, final_norm):
    tm = x_ref.shape[0]

    def issue(k, c):
        pltpu.make_async_copy(y_hbm.at[pl.ds(pos_ref[0, 0, k], 1)], buf.at[0, pl.ds(k, 1)], sem).start()
        pltpu.make_async_copy(y_hbm.at[pl.ds(pos_ref[0, 1, k], 1)], buf.at[1, pl.ds(k, 1)], sem).start()
        return c

    lax.fori_loop(0, tm, issue, 0, unroll=8)
    pltpu.make_async_copy(y_hbm.at[pl.ds(0, tm)], buf.at[0], sem).wait()
    pltpu.make_async_copy(y_hbm.at[pl.ds(0, tm)], buf.at[1], sem).wait()
    p = p_ref[...]
    out = x_ref[...] + p[:, 0:1] * buf[0] + p[:, 1:2] * buf[1]
    if final_norm:
        out = _rms(out, g_ref[...])
    o_ref[...] = out


def moe_combine(x, ys, pos3, prob, final_gain):
    t, d = x.shape
    final_norm = final_gain is not None
    gain = final_gain if final_norm else jnp.ones((d,), F32)
    tm = pos3.shape[2]
    return pl.pallas_call(
        functools.partial(_combine_kernel, final_norm=final_norm),
        grid=(t // tm,),
        in_specs=[pl.BlockSpec((1, 2, tm), lambda i: (i, 0, 0), memory_space=pltpu.SMEM),
                  pl.BlockSpec((tm, d), lambda i: (i, 0)),
                  pl.BlockSpec((tm, LANE), lambda i: (i, 0)),
                  _const_spec((1, d)),
                  pl.BlockSpec(memory_space=pl.ANY)],
        out_specs=pl.BlockSpec((tm, d), lambda i: (i, 0)),
        out_shape=jax.ShapeDtypeStruct((t, d), F32),
        scratch_shapes=[pltpu.VMEM((2, tm, d), F32), pltpu.SemaphoreType.DMA(())],
        compiler_params=_cparams(1), name="moe_combine",
    )(pos3, x, prob, gain.reshape(1, d), ys)


def _final_norm_kernel(x_ref, g_ref, o_ref):
    o_ref[...] = _rms(x_ref[...], g_ref[...])


def final_norm(x, gain, tm=ROW_TILE):
    t, d = x.shape
    return pl.pallas_call(
        _final_norm_kernel, grid=(t // tm,),
        in_specs=[pl.BlockSpec((tm, d), lambda i: (i, 0)), _const_spec((1, d))],
        out_specs=pl.BlockSpec((tm, d), lambda i: (i, 0)),
        out_shape=jax.ShapeDtypeStruct((t, d), F32),
        compiler_params=_cparams(1), name="final_norm",
    )(x, gain.reshape(1, d))


def _route_plan(idx0, idx1, tm, tc):
    t = idx0.shape[0]
    e_flat = jnp.concatenate([idx0, idx1])
    onehot = (jnp.arange(N_EXPERTS, dtype=I32)[:, None] == e_flat[None, :]).astype(I32)
    csum = jnp.cumsum(onehot, axis=1)
    counts = csum[:, -1]
    padded = ((counts + tm - 1) // tm) * tm
    ends = jnp.cumsum(padded)
    starts = ends - padded
    pos = jnp.sum(onehot * (csum - 1 + starts[:, None]), axis=0)
    n_rows = 2 * t + N_EXPERTS * tm
    n_used = (ends[-1] // tm).astype(I32)
    tile_start = jnp.arange(n_rows // tm, dtype=I32) * tm
    tile_start = jnp.minimum(tile_start, ends[-1] - 1)
    tile_expert = jnp.sum((ends[None, :] <= tile_start[:, None]).astype(I32), axis=1)
    tile_expert = jnp.minimum(tile_expert, N_EXPERTS - 1)
    pos3 = jnp.stack([pos[:t].reshape(t // tc, tc), pos[t:].reshape(t // tc, tc)], axis=1)
    return pos3, tile_expert, n_used.reshape(1), n_rows


def moe_layer(x, proj, gain, router, wg, wu, wd, layer, final_gain, seed=None, tm=MOE_TILE, tc=DISPATCH_ROWS):
    x, idx, prob = router_top2(x, proj[0], proj[1], gain, router)
    pos3, tile_expert, n_used, n_rows = _route_plan(idx[:, 0], idx[:, 1], tm, tc)
    if seed is None:
        seed = jnp.zeros((n_rows, x.shape[1]), x.dtype)
    xs = dispatch_rows(x, pos3, seed)
    ys = moe_ffn(xs, gain, wg, wu, wd, layer, tile_expert, n_used, tm)
    return moe_combine(x, ys, pos3, prob, final_gain), xs


def gla_layer(x, gain, w_in, w_gate_up, b_gate, gnorm, w_out, batch, seq):
    d = x.shape[1]
    n_main = w_in.shape[1] - GLA_GATE_RANK
    w_all = jnp.zeros((d, n_main + LANE), BF16).at[:, :w_in.shape[1]].set(w_in.astype(BF16))
    w_gu = jnp.zeros((LANE, w_gate_up.shape[1]), F32).at[:GLA_GATE_RANK].set(w_gate_up)
    qkvg, la = gla_proj(x, gain, w_all, w_gu, b_gate)
    og = gla_core(qkvg, la, gnorm, batch, seq, rows=min(GLA_ROWS, seq))
    return og, w_out.astype(BF16)


def sba_layer(x, gain, w_in, w_out, batch, seq):
    qkv = norm_proj(x, gain, w_in.astype(BF16))
    return sba_core(qkv, batch, seq), w_out.astype(BF16)


def kernel(x, norm_mix, norm_ffn, norm_final, gla_w_in, gla_w_gate_up, gla_b_gate, gla_norm, gla_w_out, sba_w_in, sba_w_out, pool_w_in, pool_w_group, pool_scale, ffn_w_gate, ffn_w_up, ffn_w_down, moe_router, moe_w_gate, moe_w_up, moe_w_down):
    batch, seq, d = x.shape
    depth = norm_mix.shape[0]
    x = x.reshape(batch * seq, d)
    normed = False
    sorted_rows = None
    moe_wg, moe_wu, moe_wd = moe_w_gate.astype(BF16), moe_w_up.astype(BF16), moe_w_down.astype(BF16)
    for i in range(depth):
        kind, j = i % 3, i // 3
        if kind == 0:
            proj = gla_layer(x, norm_mix[i], gla_w_in[j], gla_w_gate_up[j], gla_b_gate[j],
                             gla_norm[j], gla_w_out[j], batch, seq)
        elif kind == 1:
            proj = sba_layer(x, norm_mix[i], sba_w_in[j], sba_w_out[j], batch, seq)
        else:
            proj = None
            x = pool_mixer(x, norm_mix[i], pool_w_in[j].astype(BF16), pool_w_group[j].astype(BF16),
                           pool_scale[j], seq)
        f = i // 2
        if i % 2 == 0:
            x = ffn_dense(x, proj, norm_ffn[i], ffn_w_gate[f].astype(BF16), ffn_w_up[f].astype(BF16),
                          ffn_w_down[f].astype(BF16))
        else:
            if proj is None:
                raise NotImplementedError("pooling mixer followed by the routed block")
            last = i == depth - 1
            x, sorted_rows = moe_layer(x, proj, norm_ffn[i], moe_router[f], moe_wg, moe_wu, moe_wd, f,
                                       norm_final if last else None, seed=sorted_rows)
            normed = last
    if not normed:
        x = final_norm(x, norm_final)
    return x.reshape(batch, seq, d)
```

```python
import functools

import numpy as np
import jax
import jax.numpy as jnp
from jax import lax
from jax.experimental import pallas as pl
from jax.experimental.pallas import tpu as pltpu

F32 = jnp.float32
BF16 = jnp.bfloat16
I32 = jnp.int32
HIGHEST = lax.Precision.HIGHEST

RMS_EPS = 1e-6
GLA_HEADS = 4
GLA_GATE_RANK = 16
GLA_GATE_NORMALIZER = 16.0
SBA_HEADS = 16
POOL_WINDOWS = (2, 4, 8, 16)
N_EXPERTS = 8

LANE = 128
VMEM_LIMIT = 56 * 1024 * 1024
ROW_TILE = 512
FF_CHUNK = 256
GLA_CHUNK = 64
GLA_SUB = 16
GLA_ROWS = 1024
GLA_SAFE_EXP = 60.0
SBA_BQ = 256
SBA_ROWS = 2048
SBA_DEAD2 = 150.0
LOG2E = 1.4426950408889634
MOE_TILE = 512
DISPATCH_ROWS = 1024


def _cparams(n_axes):
    return pltpu.CompilerParams(dimension_semantics=("arbitrary",) * n_axes,
                                vmem_limit_bytes=VMEM_LIMIT)


def _rms(xf, gain):
    return xf * lax.rsqrt(jnp.mean(xf * xf, axis=-1, keepdims=True) + RMS_EPS) * gain


def _sigmoid(x):
    return 1.0 / (1.0 + jnp.exp(-x))


def _log_sigmoid(z):
    return jnp.minimum(z, 0.0) - jnp.log(1.0 + jnp.exp(-jnp.abs(z)))


def _dot_split(a, b, split_lhs=False):
    f = a if split_lhs else b
    hi = f.astype(BF16)
    lo = (f - hi.astype(F32)).astype(BF16)
    if split_lhs:
        return (jnp.dot(hi, b, preferred_element_type=F32) + jnp.dot(lo, b, preferred_element_type=F32))
    return (jnp.dot(a, hi, preferred_element_type=F32) + jnp.dot(a, lo, preferred_element_type=F32))


def _const_spec(shape):
    return pl.BlockSpec(shape, lambda *_: (0,) * len(shape))


def _norm_proj_kernel(x_ref, g_ref, w_ref, o_ref, *, nchunk):
    h = _rms(x_ref[...], g_ref[...]).astype(BF16)
    n = o_ref.shape[1]
    for c0 in range(0, n, nchunk):
        o_ref[:, c0:c0 + nchunk] = jnp.dot(
            h, w_ref[:, c0:c0 + nchunk], preferred_element_type=F32).astype(o_ref.dtype)


def norm_proj(x, gain, w, tm=ROW_TILE):
    t, d = x.shape
    n = w.shape[1]
    return pl.pallas_call(
        functools.partial(_norm_proj_kernel, nchunk=512),
        grid=(t // tm,),
        in_specs=[pl.BlockSpec((tm, d), lambda i: (i, 0)),
                  _const_spec((1, d)), _const_spec((d, n))],
        out_specs=pl.BlockSpec((tm, n), lambda i: (i, 0)),
        out_shape=jax.ShapeDtypeStruct((t, n), BF16),
        compiler_params=_cparams(1), name="norm_proj",
    )(x, gain.reshape(1, d), w)


def _gla_proj_kernel(x_ref, g_ref, w_ref, wgu_ref, bg_ref, o_ref, la_ref, *, nchunk):
    h = _rms(x_ref[...], g_ref[...]).astype(BF16)
    n = o_ref.shape[1]
    for c0 in range(0, n, nchunk):
        o_ref[:, c0:c0 + nchunk] = jnp.dot(
            h, w_ref[:, c0:c0 + nchunk], preferred_element_type=F32).astype(o_ref.dtype)
    a = jnp.dot(h, w_ref[:, n:], preferred_element_type=F32)
    z = jnp.dot(a, wgu_ref[...], precision=HIGHEST, preferred_element_type=F32) + bg_ref[...]
    la_ref[...] = _log_sigmoid(z) / GLA_GATE_NORMALIZER


def gla_proj(x, gain, w_all, w_gu, b_gate, tm=ROW_TILE):
    t, d = x.shape
    n = w_all.shape[1] - LANE
    dk = w_gu.shape[1]
    return pl.pallas_call(
        functools.partial(_gla_proj_kernel, nchunk=512),
        grid=(t // tm,),
        in_specs=[pl.BlockSpec((tm, d), lambda i: (i, 0)),
                  _const_spec((1, d)), _const_spec((d, n + LANE)),
                  _const_spec((LANE, dk)), _const_spec((1, dk))],
        out_specs=[pl.BlockSpec((tm, n), lambda i: (i, 0)),
                   pl.BlockSpec((tm, dk), lambda i: (i, 0))],
        out_shape=[jax.ShapeDtypeStruct((t, n), BF16), jax.ShapeDtypeStruct((t, dk), F32)],
        compiler_params=_cparams(1), name="gla_proj",
    )(x, gain.reshape(1, d), w_all, w_gu, b_gate.reshape(1, dk))


def _gla_tables():
    c, s = GLA_CHUNK, GLA_SUB
    i = np.arange(c)[:, None]
    j = np.arange(c)[None, :]
    lo = (i // s) * s
    hi = lo + s
    a1 = (j >= lo) & (j <= i)
    a2 = (j > i) & (j < hi)
    a3 = j < lo
    a4 = j >= hi
    amat = np.concatenate([a1, a2, a3, a4], axis=0).astype(np.float32)
    row = np.arange(s * LANE)[:, None] // LANE
    erep = (row == (np.arange(c)[None, :] % s)).astype(np.float32)
    return jnp.asarray(amat, dtype=BF16), jnp.asarray(erep, dtype=BF16)


def _gla_kernel(q_ref, k_ref, v_ref, g_ref, la_ref, gn_ref, amat_ref, erep_ref,
                og_ref, state_s, kf_s, dq_s, tm_s, *, qscale):
    c, s = GLA_CHUNK, GLA_SUB
    nsub = c // s
    nh = state_s.shape[0]
    dk = q_ref.shape[1] // nh
    dv = v_ref.shape[1] // nh

    @pl.when(pl.program_id(1) == 0)
    def _():
        state_s[...] = jnp.zeros_like(state_s)

    il = lax.broadcasted_iota(I32, (s, dk), 0)
    ii = lax.broadcasted_iota(I32, (c, c), 0)
    jj = lax.broadcasted_iota(I32, (c, c), 1)
    ri, rj = ii // s, jj // s
    on_diag = jnp.logical_and(ri == rj, ii >= jj)
    rowblk = lax.broadcasted_iota(I32, (c, dk), 0) // s
    ones_ck = jnp.ones((c, dk), BF16)

    def head_chunk(h, r0, exact_diag):
        kcols = slice(h * dk, (h + 1) * dk)
        vcols = slice(h * dv, (h + 1) * dv)
        la = la_ref[pl.ds(r0, c), kcols]
        qf = q_ref[pl.ds(r0, c), kcols].astype(F32) * qscale
        kf = k_ref[pl.ds(r0, c), kcols].astype(F32)
        vb = v_ref[pl.ds(r0, c), vcols]
        cum = _dot_split(amat_ref[...], la)
        dq, dkk, cprev, ctail = cum[0:c], cum[c:2 * c], cum[2 * c:3 * c], cum[3 * c:4 * c]
        cnext = cprev + dq + dkk
        qt = qf * jnp.exp(dq)
        kt = kf * jnp.exp(dkk)
        if exact_diag:
            kf_s[h] = kf
            dq_s[h] = dq
            for r in range(nsub):
                qb = qf[r * s:(r + 1) * s]
                dqb = dq[r * s:(r + 1) * s]
                for jl in range(s):
                    krow = kf_s[h, pl.ds(r * s + jl, 1), :]
                    drow = dq_s[h, pl.ds(r * s + jl, 1), :]
                    dec = jnp.where(il >= jl, jnp.exp(jnp.minimum(dqb - drow, 0.0)), 0.0)
                    tm_s[h, r * s:(r + 1) * s, jl * dk:(jl + 1) * dk] = (qb * krow * dec).astype(BF16)
            sdiag = jnp.dot(tm_s[h], erep_ref[...], preferred_element_type=F32)
        else:
            kd = (kf * jnp.exp(-dq)).astype(BF16)
            sdiag = lax.dot_general(qt.astype(BF16), kd, (((1,), (1,)), ((), ())),
                                    preferred_element_type=F32)
        soff = jnp.zeros((c, c), F32)
        for rp in range(nsub - 1):
            cn = cnext[rp * s:rp * s + 1, :]
            p = (qt * jnp.exp(jnp.minimum(cprev - cn, 0.0))).astype(BF16)
            kp = jnp.where(rowblk == rp, kt, 0.0).astype(BF16)
            soff = soff + lax.dot_general(p, kp, (((1,), (1,)), ((), ())),
                                          preferred_element_type=F32)
        sc = jnp.where(on_diag, sdiag, jnp.where(ri > rj, soff, 0.0))
        st = state_s[h]
        o = jnp.dot(sc.astype(BF16), vb, preferred_element_type=F32) + jnp.dot(
            (qt * jnp.exp(cprev)).astype(BF16), st.astype(BF16), preferred_element_type=F32)
        ks_t = (kt * jnp.exp(ctail)).T.astype(BF16)
        upd = jnp.dot(ks_t, vb, preferred_element_type=F32)
        dcol = jnp.exp(_dot_split(la.T, ones_ck, split_lhs=True))
        state_s[h] = jnp.concatenate([dcol] * (dv // dk), axis=1) * st + upd
        gf = g_ref[pl.ds(r0, c), vcols].astype(F32)
        og_ref[pl.ds(r0, c), vcols] = (_rms(o, gn_ref[...]) * (gf * _sigmoid(gf))).astype(og_ref.dtype)

    def sweep(exact_diag):
        def chunk(ci, carry):
            r0 = pl.multiple_of(ci * c, c)
            for h in range(nh):
                head_chunk(h, r0, exact_diag)
            return carry

        lax.fori_loop(0, q_ref.shape[0] // c, chunk, 0, unroll=1 if exact_diag else 2)

    safe = jnp.max(-la_ref[...]) * s < GLA_SAFE_EXP
    pl.when(safe)(lambda: sweep(False))
    pl.when(jnp.logical_not(safe))(lambda: sweep(True))


def gla_core(qkvg, la, gnorm, batch, seq, rows=GLA_ROWS):
    t = qkvg.shape[0]
    dkt = la.shape[1]
    dk = dkt // GLA_HEADS
    dvt = (qkvg.shape[1] - 2 * dkt) // 2
    dv = dvt // GLA_HEADS
    nblk = seq // rows
    amat, erep = _gla_tables()
    row = lambda b, c: b * nblk + c
    return pl.pallas_call(
        functools.partial(_gla_kernel, qscale=float(dk) ** -0.5),
        grid=(batch, nblk),
        in_specs=[pl.BlockSpec((rows, dkt), lambda b, c: (row(b, c), 0)),
                  pl.BlockSpec((rows, dkt), lambda b, c: (row(b, c), 1)),
                  pl.BlockSpec((rows, dvt), lambda b, c: (row(b, c), 2 * dkt // dvt)),
                  pl.BlockSpec((rows, dvt), lambda b, c: (row(b, c), 2 * dkt // dvt + 1)),
                  pl.BlockSpec((rows, dkt), lambda b, c: (row(b, c), 0)),
                  _const_spec((1, dv)), _const_spec(amat.shape), _const_spec(erep.shape)],
        out_specs=pl.BlockSpec((rows, dvt), lambda b, c: (row(b, c), 0)),
        out_shape=jax.ShapeDtypeStruct((t, dvt), BF16),
        scratch_shapes=[pltpu.VMEM((GLA_HEADS, dk, dv), F32),
                        pltpu.VMEM((GLA_HEADS, GLA_CHUNK, dk), F32),
                        pltpu.VMEM((GLA_HEADS, GLA_CHUNK, dk), F32),
                        pltpu.VMEM((GLA_HEADS, GLA_CHUNK, GLA_SUB * dk), BF16)],
        compiler_params=_cparams(2), name="gla_core",
    )(qkvg, qkvg, qkvg, qkvg, la, gnorm.reshape(1, dv), amat, erep)


def _sba_kernel(q_ref, k_ref, v_ref, u_ref, o_ref, qs_s, acc_s, carry_s, *, scale, dh):
    bq = SBA_BQ
    nq = q_ref.shape[0] // bq
    step = pl.program_id(2)
    lane = lax.broadcasted_iota(I32, (bq, LANE), 1)
    row = lax.broadcasted_iota(I32, (2 * bq, bq), 0)
    strict = lax.broadcasted_iota(I32, (2 * bq, bq), 1) < jnp.where(row >= bq, row - bq, row)

    def block(jb, diagonal):
        k0 = pl.multiple_of(jb * bq, bq)
        kb = k_ref[pl.ds(k0, bq), :]
        vb = v_ref[pl.ds(k0, bq), :]
        z = lax.dot_general(qs_s[...], kb, (((1,), (1,)), ((), ())), preferred_element_type=F32)
        sp = jnp.maximum(z, 0.0) + jnp.log2(1.0 + jnp.exp2(jnp.minimum(z, -z)))
        if diagonal:
            sp = jnp.where(strict, sp, 0.0)
        ct = jnp.dot(sp.astype(BF16), u_ref[...], preferred_element_type=F32)
        carry = carry_s[...]
        w = jnp.exp2((z - sp) - ct[:, :bq] - jnp.concatenate([carry] * (bq // LANE), axis=1))
        if diagonal:
            w = jnp.where(strict, w, 0.0)
        acc_s[...] += jnp.dot(w.astype(BF16), vb, preferred_element_type=F32)
        carry = carry + ct[:, bq:]
        carry_s[...] = carry
        return jnp.min(carry)

    def qblock(qi, c):
        r0 = pl.multiple_of(qi * bq, bq)
        q2 = q_ref[pl.ds(r0, bq), :].astype(F32) * scale
        qs_s[0:bq] = jnp.where(lane < dh, q2, 0.0).astype(BF16)
        qs_s[bq:2 * bq] = jnp.where(lane >= dh, q2, 0.0).astype(BF16)
        acc_s[...] = jnp.zeros_like(acc_s)
        carry_s[...] = jnp.zeros_like(carry_s)
        jd = step * nq + qi
        low = block(jd, True)
        lax.while_loop(lambda st: jnp.logical_and(st[0] >= 0, st[1] < SBA_DEAD2),
                       lambda st: (st[0] - 1, block(st[0], False)), (jd - 1, low))
        a = acc_s[...]
        o_ref[pl.ds(r0, bq), :] = jnp.where(lane < dh, a[:bq], a[bq:]).astype(o_ref.dtype)
        return c

    lax.fori_loop(0, nq, qblock, 0)


def sba_core(qkv, batch, seq):
    t = qkv.shape[0]
    d = qkv.shape[1] // 3
    dh = d // SBA_HEADS
    npair = d // LANE
    bq = SBA_BQ
    rows = min(SBA_ROWS, seq)
    nstep = seq // rows
    u = np.concatenate([np.tril(np.ones((bq, bq), np.float32), -1), np.ones((bq, LANE), np.float32)], 1)
    return pl.pallas_call(
        functools.partial(_sba_kernel, scale=float(dh) ** -0.5 * LOG2E, dh=dh),
        grid=(batch, npair, nstep),
        in_specs=[pl.BlockSpec((rows, LANE), lambda b, p, i: (b * nstep + i, p)),
                  pl.BlockSpec((seq, LANE), lambda b, p, i: (b, npair + p)),
                  pl.BlockSpec((seq, LANE), lambda b, p, i: (b, 2 * npair + p)),
                  _const_spec((bq, bq + LANE))],
        out_specs=pl.BlockSpec((rows, LANE), lambda b, p, i: (b * nstep + i, p)),
        out_shape=jax.ShapeDtypeStruct((t, d), BF16),
        scratch_shapes=[pltpu.VMEM((2 * bq, LANE), BF16), pltpu.VMEM((2 * bq, LANE), F32),
                        pltpu.VMEM((2 * bq, LANE), F32)],
        compiler_params=_cparams(3), name="sba_core",
    )(qkv, qkv, qkv, jnp.asarray(u, dtype=BF16))


def _pool_kernel(x_ref, halo_ref, g_ref, win_ref, wgrp_ref, sc_ref, o_ref, u_s, *, tiles_per_seq):
    tm, d = x_ref.shape
    hal = halo_ref.shape[0]
    gd = wgrp_ref.shape[1]
    first = (pl.program_id(0) % tiles_per_seq) == 0
    x = x_ref[...]
    xx = jnp.concatenate([halo_ref[...], x], axis=0)
    h = _rms(xx, g_ref[...]).astype(BF16)
    u = jnp.dot(h, win_ref[...], preferred_element_type=F32)
    keep = jnp.where(first, 0.0, 1.0)
    rowi = lax.broadcasted_iota(I32, (hal + tm, 1), 0)
    u_s[...] = u * jnp.where(rowi < hal, keep, 1.0)
    pos = (pl.program_id(0) % tiles_per_seq) * tm + lax.broadcasted_iota(I32, (tm, 1), 0)
    mixed = []
    for gi, w in enumerate(POOL_WINDOWS):
        cs = slice(gi * gd, (gi + 1) * gd)
        ssum = u_s[hal:hal + tm, cs]
        for dlt in range(1, w):
            ssum = ssum + u_s[hal - dlt:hal - dlt + tm, cs]
        cnt = jnp.minimum(pos + 1, w).astype(F32)
        pooled = ssum / cnt - u_s[hal:hal + tm, cs]
        mixed.append(jnp.dot(pooled.astype(BF16), wgrp_ref[gi], preferred_element_type=F32))
    o_ref[...] = x + jnp.concatenate(mixed, axis=1) * sc_ref[...]


def pool_mixer(x, gain, w_in, w_group, scale, seq, tm=ROW_TILE):
    t, d = x.shape
    hal = max(POOL_WINDOWS)
    g, gd, _ = w_group.shape
    return pl.pallas_call(
        functools.partial(_pool_kernel, tiles_per_seq=seq // tm),
        grid=(t // tm,),
        in_specs=[pl.BlockSpec((tm, d), lambda i: (i, 0)),
                  pl.BlockSpec((hal, d), lambda i: (jnp.maximum(i * (tm // hal) - 1, 0), 0)),
                  _const_spec((1, d)), _const_spec((d, d)), _const_spec((g, gd, gd)),
                  _const_spec((1, d))],
        out_specs=pl.BlockSpec((tm, d), lambda i: (i, 0)),
        out_shape=jax.ShapeDtypeStruct((t, d), F32),
        scratch_shapes=[pltpu.VMEM((hal + tm, d), F32)],
        compiler_params=_cparams(1), name="pool_mixer",
    )(x, x, gain.reshape(1, d), w_in, w_group, scale.reshape(1, d))


def _swiglu_acc(h, wg_ref, wu_ref, wd_ref, acc, widx):
    ff = wg_ref.shape[-1]
    for c0 in range(0, ff, FF_CHUNK):
        cs = slice(c0, c0 + FF_CHUNK)
        gate = jnp.dot(h, wg_ref[widx + (slice(None), cs)], preferred_element_type=F32)
        up = jnp.dot(h, wu_ref[widx + (slice(None), cs)], preferred_element_type=F32)
        a = (gate * _sigmoid(gate) * up).astype(BF16)
        acc = acc + jnp.dot(a, wd_ref[widx + (cs, slice(None))], preferred_element_type=F32)
    return acc


def _ffn_kernel(x_ref, a_ref, wo_ref, g_ref, wg_ref, wu_ref, wd_ref, o_ref):
    x = x_ref[...] + jnp.dot(a_ref[...], wo_ref[...], preferred_element_type=F32)
    h = _rms(x, g_ref[...]).astype(BF16)
    o_ref[...] = _swiglu_acc(h, wg_ref, wu_ref, wd_ref, x, ())


def _ffn_plain_kernel(x_ref, g_ref, wg_ref, wu_ref, wd_ref, o_ref):
    x = x_ref[...]
    h = _rms(x, g_ref[...]).astype(BF16)
    o_ref[...] = _swiglu_acc(h, wg_ref, wu_ref, wd_ref, x, ())


def ffn_dense(x, proj, gain, wg, wu, wd, tm=ROW_TILE):
    t, d = x.shape
    ff = wg.shape[1]
    row_spec = pl.BlockSpec((tm, d), lambda i: (i, 0))
    w_specs = [_const_spec((1, d)), _const_spec((d, ff)), _const_spec((d, ff)), _const_spec((ff, d))]
    if proj is None:
        body, in_specs, args = _ffn_plain_kernel, [row_spec], (x,)
    else:
        a, w_out = proj
        ka = a.shape[1]
        body, args = _ffn_kernel, (x, a, w_out)
        in_specs = [row_spec, pl.BlockSpec((tm, ka), lambda i: (i, 0)), _const_spec((ka, d))]
    return pl.pallas_call(
        body,
        grid=(t // tm,),
        in_specs=in_specs + w_specs,
        out_specs=row_spec,
        out_shape=jax.ShapeDtypeStruct((t, d), F32),
        compiler_params=_cparams(1), name="ffn_dense",
    )(*args, gain.reshape(1, d), wg, wu, wd)


def _moe_ffn_kernel(texp_ref, nused_ref, x_ref, g_ref, wg_ref, wu_ref, wd_ref, o_ref):
    live = pl.program_id(0) < nused_ref[0]

    @pl.when(live)
    def _():
        h = _rms(x_ref[...], g_ref[...]).astype(BF16)
        o_ref[...] = _swiglu_acc(h, wg_ref, wu_ref, wd_ref, jnp.zeros(o_ref.shape, F32), (0, 0))

    @pl.when(jnp.logical_not(live))
    def _():
        o_ref[...] = jnp.zeros_like(o_ref)


def moe_ffn(xs, gain, wg, wu, wd, layer, tile_expert, n_used, tm):
    r, d = xs.shape
    ff = wg.shape[3]
    live = lambda i, te, nu: jnp.minimum(i, nu[0] - 1)
    grid_spec = pltpu.PrefetchScalarGridSpec(
        num_scalar_prefetch=2,
        grid=(r // tm,),
        in_specs=[pl.BlockSpec((tm, d), lambda i, te, nu: (live(i, te, nu), 0)),
                  pl.BlockSpec((1, d), lambda i, te, nu: (0, 0)),
                  pl.BlockSpec((1, 1, d, ff), lambda i, te, nu: (layer, te[i], 0, 0)),
                  pl.BlockSpec((1, 1, d, ff), lambda i, te, nu: (layer, te[i], 0, 0)),
                  pl.BlockSpec((1, 1, ff, d), lambda i, te, nu: (layer, te[i], 0, 0))],
        out_specs=pl.BlockSpec((tm, d), lambda i, te, nu: (i, 0)),
    )
    return pl.pallas_call(
        _moe_ffn_kernel, grid_spec=grid_spec,
        out_shape=jax.ShapeDtypeStruct((r, d), F32),
        compiler_params=_cparams(1), name="moe_ffn",
    )(tile_expert, n_used, xs, gain.reshape(1, d), wg, wu, wd)


def _router_kernel(x_ref, a_ref, wo_ref, g_ref, wr_ref, xo_ref, idx_ref, prob_ref, *, ne):
    x = x_ref[...] + jnp.dot(a_ref[...], wo_ref[...], preferred_element_type=F32)
    xo_ref[...] = x
    h = _rms(x, g_ref[...])
    h_hi = h.astype(BF16)
    h_lo = (h - h_hi.astype(F32)).astype(BF16)
    wr = wr_ref[...]
    w_hi = wr.astype(BF16)
    w_lo = (wr - w_hi.astype(F32)).astype(BF16)
    logits = (jnp.dot(h_hi, w_hi, preferred_element_type=F32)
              + jnp.dot(h_lo, w_hi, preferred_element_type=F32)
              + jnp.dot(h_hi, w_lo, preferred_element_type=F32))
    col = lax.broadcasted_iota(I32, logits.shape, 1).astype(F32)
    logits = jnp.where(col < ne, logits, -jnp.inf)
    m1 = jnp.max(logits, axis=1, keepdims=True)
    i1 = jnp.min(jnp.where(logits == m1, col, float(ne)), axis=1, keepdims=True)
    rest = jnp.where(col == i1, -jnp.inf, logits)
    m2 = jnp.max(rest, axis=1, keepdims=True)
    i2 = jnp.min(jnp.where(rest == m2, col, float(ne)), axis=1, keepdims=True)
    e2 = jnp.exp(m2 - m1)
    p1 = 1.0 / (1.0 + e2)
    idx_ref[...] = jnp.where(col == 0.0, i1, i2).astype(I32)
    prob_ref[...] = jnp.where(col == 0.0, p1, e2 * p1)


def router_top2(x, a, w_out, gain, wr, tm=ROW_TILE):
    t, d = x.shape
    ka = a.shape[1]
    ne = wr.shape[1]
    wr_pad = jnp.zeros((d, LANE), F32).at[:, :ne].set(wr)
    return pl.pallas_call(
        functools.partial(_router_kernel, ne=ne),
        grid=(t // tm,),
        in_specs=[pl.BlockSpec((tm, d), lambda i: (i, 0)), pl.BlockSpec((tm, ka), lambda i: (i, 0)),
                  _const_spec((ka, d)), _const_spec((1, d)), _const_spec((d, LANE))],
        out_specs=[pl.BlockSpec((tm, d), lambda i: (i, 0)),
                   pl.BlockSpec((tm, LANE), lambda i: (i, 0)), pl.BlockSpec((tm, LANE), lambda i: (i, 0))],
        out_shape=[jax.ShapeDtypeStruct((t, d), F32),
                   jax.ShapeDtypeStruct((t, LANE), I32), jax.ShapeDtypeStruct((t, LANE), F32)],
        compiler_params=_cparams(1), name="router_top2",
    )(x, a, w_out, gain.reshape(1, d), wr_pad)


def _dispatch_kernel(pos_ref, x_ref, xs_in, xs_hbm, sem):
    del xs_in
    tm = x_ref.shape[0]

    def issue(k, c):
        pltpu.make_async_copy(x_ref.at[pl.ds(k, 1)], xs_hbm.at[pl.ds(pos_ref[0, 0, k], 1)], sem).start()
        pltpu.make_async_copy(x_ref.at[pl.ds(k, 1)], xs_hbm.at[pl.ds(pos_ref[0, 1, k], 1)], sem).start()
        return c

    lax.fori_loop(0, tm, issue, 0, unroll=8)
    pltpu.make_async_copy(x_ref, xs_hbm.at[pl.ds(0, tm)], sem).wait()
    pltpu.make_async_copy(x_ref, xs_hbm.at[pl.ds(0, tm)], sem).wait()


def dispatch_rows(x, pos3, seed):
    t, d = x.shape
    n_rows = seed.shape[0]
    tm = pos3.shape[2]
    return pl.pallas_call(
        _dispatch_kernel,
        grid=(t // tm,),
        in_specs=[pl.BlockSpec((1, 2, tm), lambda i: (i, 0, 0), memory_space=pltpu.SMEM),
                  pl.BlockSpec((tm, d), lambda i: (i, 0)),
                  pl.BlockSpec(memory_space=pl.ANY)],
        out_specs=pl.BlockSpec(memory_space=pl.ANY),
        out_shape=jax.ShapeDtypeStruct((n_rows, d), x.dtype),
        input_output_aliases={2: 0},
        scratch_shapes=[pltpu.SemaphoreType.DMA(())],
        compiler_params=_cparams(1), name="dispatch_rows",
    )(pos3, x, seed)


def _combine_kernel(pos_ref, x_ref, p_ref, g_ref, y_hbm, o_ref, buf, sem, *, final_norm):
    tm = x_ref.shape[0]

    def issue(k, c):
        pltpu.make_async_copy(y_hbm.at[pl.ds(pos_ref[0, 0, k], 1)], buf.at[0, pl.ds(k, 1)], sem).start()
        pltpu.make_async_copy(y_hbm.at[pl.ds(pos_ref[0, 1, k], 1)], buf.at[1, pl.ds(k, 1)], sem).start()
        return c

    lax.fori_loop(0, tm, issue, 0, unroll=8)
    pltpu.make_async_copy(y_hbm.at[pl.ds(0, tm)], buf.at[0], sem).wait()
    pltpu.make_async_copy(y_hbm.at[pl.ds(0, tm)], buf.at[1], sem).wait()
    p = p_ref[...]
    out = x_ref[...] + p[:, 0:1] * buf[0] + p[:, 1:2] * buf[1]
    if final_norm:
        out = _rms(out, g_ref[...])
    o_ref[...] = out


def moe_combine(x, ys, pos3, prob, final_gain):
    t, d = x.shape
    final_norm = final_gain is not None
    gain = final_gain if final_norm else jnp.ones((d,), F32)
    tm = pos3.shape[2]
    return pl.pallas_call(
        functools.partial(_combine_kernel, final_norm=final_norm),
        grid=(t // tm,),
        in_specs=[pl.BlockSpec((1, 2, tm), lambda i: (i, 0, 0), memory_space=pltpu.SMEM),
                  pl.BlockSpec((tm, d), lambda i: (i, 0)),
                  pl.BlockSpec((tm, LANE), lambda i: (i, 0)),
                  _const_spec((1, d)),
                  pl.BlockSpec(memory_space=pl.ANY)],
        out_specs=pl.BlockSpec((tm, d), lambda i: (i, 0)),
        out_shape=jax.ShapeDtypeStruct((t, d), F32),
        scratch_shapes=[pltpu.VMEM((2, tm, d), F32), pltpu.SemaphoreType.DMA(())],
        compiler_params=_cparams(1), name="moe_combine",
    )(pos3, x, prob, gain.reshape(1, d), ys)


def _final_norm_kernel(x_ref, g_ref, o_ref):
    o_ref[...] = _rms(x_ref[...], g_ref[...])


def final_norm(x, gain, tm=ROW_TILE):
    t, d = x.shape
    return pl.pallas_call(
        _final_norm_kernel, grid=(t // tm,),
        in_specs=[pl.BlockSpec((tm, d), lambda i: (i, 0)), _const_spec((1, d))],
        out_specs=pl.BlockSpec((tm, d), lambda i: (i, 0)),
        out_shape=jax.ShapeDtypeStruct((t, d), F32),
        compiler_params=_cparams(1), name="final_norm",
    )(x, gain.reshape(1, d))


def _route_plan(idx0, idx1, tm, tc):
    t = idx0.shape[0]
    e_flat = jnp.concatenate([idx0, idx1])
    onehot = (jnp.arange(N_EXPERTS, dtype=I32)[:, None] == e_flat[None, :]).astype(I32)
    csum = jnp.cumsum(onehot, axis=1)
    counts = csum[:, -1]
    padded = ((counts + tm - 1) // tm) * tm
    ends = jnp.cumsum(padded)
    starts = ends - padded
    pos = jnp.sum(onehot * (csum - 1 + starts[:, None]), axis=0)
    n_rows = 2 * t + N_EXPERTS * tm
    n_used = (ends[-1] // tm).astype(I32)
    tile_start = jnp.arange(n_rows // tm, dtype=I32) * tm
    tile_start = jnp.minimum(tile_start, ends[-1] - 1)
    tile_expert = jnp.sum((ends[None, :] <= tile_start[:, None]).astype(I32), axis=1)
    tile_expert = jnp.minimum(tile_expert, N_EXPERTS - 1)
    pos3 = jnp.stack([pos[:t].reshape(t // tc, tc), pos[t:].reshape(t // tc, tc)], axis=1)
    return pos3, tile_expert, n_used.reshape(1), n_rows


def moe_layer(x, proj, gain, router, wg, wu, wd, layer, final_gain, seed=None, tm=MOE_TILE, tc=DISPATCH_ROWS):
    x, idx, prob = router_top2(x, proj[0], proj[1], gain, router)
    pos3, tile_expert, n_used, n_rows = _route_plan(idx[:, 0], idx[:, 1], tm, tc)
    if seed is None:
        seed = jnp.zeros((n_rows, x.shape[1]), x.dtype)
    xs = dispatch_rows(x, pos3, seed)
    ys = moe_ffn(xs, gain, wg, wu, wd, layer, tile_expert, n_used, tm)
    return moe_combine(x, ys, pos3, prob, final_gain), xs


def gla_layer(x, gain, w_in, w_gate_up, b_gate, gnorm, w_out, batch, seq):
    d = x.shape[1]
    n_main = w_in.shape[1] - GLA_GATE_RANK
    w_all = jnp.zeros((d, n_main + LANE), BF16).at[:, :w_in.shape[1]].set(w_in.astype(BF16))
    w_gu = jnp.zeros((LANE, w_gate_up.shape[1]), F32).at[:GLA_GATE_RANK].set(w_gate_up)
    qkvg, la = gla_proj(x, gain, w_all, w_gu, b_gate)
    og = gla_core(qkvg, la, gnorm, batch, seq, rows=min(GLA_ROWS, seq))
    return og, w_out.astype(BF16)


def sba_layer(x, gain, w_in, w_out, batch, seq):
    qkv = norm_proj(x, gain, w_in.astype(BF16))
    return sba_core(qkv, batch, seq), w_out.astype(BF16)


def kernel(x, norm_mix, norm_ffn, norm_final, gla_w_in, gla_w_gate_up, gla_b_gate, gla_norm, gla_w_out, sba_w_in, sba_w_out, pool_w_in, pool_w_group, pool_scale, ffn_w_gate, ffn_w_up, ffn_w_down, moe_router, moe_w_gate, moe_w_up, moe_w_down):
    batch, seq, d = x.shape
    depth = norm_mix.shape[0]
    x = x.reshape(batch * seq, d)
    normed = False
    sorted_rows = None
    moe_wg, moe_wu, moe_wd = moe_w_gate.astype(BF16), moe_w_up.astype(BF16), moe_w_down.astype(BF16)
    for i in range(depth):
        kind, j = i % 3, i // 3
        if kind == 0:
            proj = gla_layer(x, norm_mix[i], gla_w_in[j], gla_w_gate_up[j], gla_b_gate[j],
                             gla_norm[j], gla_w_out[j], batch, seq)
        elif kind == 1:
            proj = sba_layer(x, norm_mix[i], sba_w_in[j], sba_w_out[j], batch, seq)
        else:
            proj = None
            x = pool_mixer(x, norm_mix[i], pool_w_in[j].astype(BF16), pool_w_group[j].astype(BF16),
                           pool_scale[j], seq)
        f = i // 2
        if i % 2 == 0:
            x = ffn_dense(x, proj, norm_ffn[i], ffn_w_gate[f].astype(BF16), ffn_w_up[f].astype(BF16),
                          ffn_w_down[f].astype(BF16))
        else:
            if proj is None:
                raise NotImplementedError("pooling mixer followed by the routed block")
            last = i == depth - 1
            x, sorted_rows = moe_layer(x, proj, norm_ffn[i], moe_router[f], moe_wg, moe_wu, moe_wd, f,
                                       norm_final if last else None, seed=sorted_rows)
            normed = last
    if not normed:
        x = final_norm(x, norm_final)
    return x.reshape(batch, seq, d)
```

```python
import functools

import numpy as np
import jax
import jax.numpy as jnp
from jax import lax
from jax.experimental import pallas as pl
from jax.experimental.pallas import tpu as pltpu

F32 = jnp.float32
BF16 = jnp.bfloat16
I32 = jnp.int32
HIGHEST = lax.Precision.HIGHEST

RMS_EPS = 1e-6
GLA_HEADS = 4
GLA_GATE_RANK = 16
GLA_GATE_NORMALIZER = 16.0
SBA_HEADS = 16
POOL_WINDOWS = (2, 4, 8, 16)
N_EXPERTS = 8

LANE = 128
VMEM_LIMIT = 56 * 1024 * 1024
ROW_TILE = 512
FF_CHUNK = 256
GLA_CHUNK = 64
GLA_SUB = 16
GLA_ROWS = 1024
GLA_SAFE_EXP = 60.0
SBA_BQ = 256
SBA_ROWS = 2048
SBA_DEAD2 = 150.0
LOG2E = 1.4426950408889634
MOE_TILE = 512
DISPATCH_ROWS = 1024


def _cparams(n_axes):
    return pltpu.CompilerParams(dimension_semantics=("arbitrary",) * n_axes,
                                vmem_limit_bytes=VMEM_LIMIT)


def _rms(xf, gain):
    return xf * lax.rsqrt(jnp.mean(xf * xf, axis=-1, keepdims=True) + RMS_EPS) * gain


def _sigmoid(x):
    return 1.0 / (1.0 + jnp.exp(-x))


def _log_sigmoid(z):
    return jnp.minimum(z, 0.0) - jnp.log(1.0 + jnp.exp(-jnp.abs(z)))


def _dot_split(a, b, split_lhs=False):
    f = a if split_lhs else b
    hi = f.astype(BF16)
    lo = (f - hi.astype(F32)).astype(BF16)
    if split_lhs:
        return (jnp.dot(hi, b, preferred_element_type=F32) + jnp.dot(lo, b, preferred_element_type=F32))
    return (jnp.dot(a, hi, preferred_element_type=F32) + jnp.dot(a, lo, preferred_element_type=F32))


def _const_spec(shape):
    return pl.BlockSpec(shape, lambda *_: (0,) * len(shape))


def _norm_proj_kernel(x_ref, g_ref, w_ref, o_ref, *, nchunk):
    h = _rms(x_ref[...], g_ref[...]).astype(BF16)
    n = o_ref.shape[1]
    for c0 in range(0, n, nchunk):
        o_ref[:, c0:c0 + nchunk] = jnp.dot(
            h, w_ref[:, c0:c0 + nchunk], preferred_element_type=F32).astype(o_ref.dtype)


def norm_proj(x, gain, w, tm=ROW_TILE):
    t, d = x.shape
    n = w.shape[1]
    return pl.pallas_call(
        functools.partial(_norm_proj_kernel, nchunk=512),
        grid=(t // tm,),
        in_specs=[pl.BlockSpec((tm, d), lambda i: (i, 0)),
                  _const_spec((1, d)), _const_spec((d, n))],
        out_specs=pl.BlockSpec((tm, n), lambda i: (i, 0)),
        out_shape=jax.ShapeDtypeStruct((t, n), BF16),
        compiler_params=_cparams(1), name="norm_proj",
    )(x, gain.reshape(1, d), w)


def _gla_proj_kernel(x_ref, g_ref, w_ref, wgu_ref, bg_ref, o_ref, la_ref, *, nchunk):
    h = _rms(x_ref[...], g_ref[...]).astype(BF16)
    n = o_ref.shape[1]
    for c0 in range(0, n, nchunk):
        o_ref[:, c0:c0 + nchunk] = jnp.dot(
            h, w_ref[:, c0:c0 + nchunk], preferred_element_type=F32).astype(o_ref.dtype)
    a = jnp.dot(h, w_ref[:, n:], preferred_element_type=F32)
    z = jnp.dot(a, wgu_ref[...], precision=HIGHEST, preferred_element_type=F32) + bg_ref[...]
    la_ref[...] = _log_sigmoid(z) / GLA_GATE_NORMALIZER


def gla_proj(x, gain, w_all, w_gu, b_gate, tm=ROW_TILE):
    t, d = x.shape
    n = w_all.shape[1] - LANE
    dk = w_gu.shape[1]
    return pl.pallas_call(
        functools.partial(_gla_proj_kernel, nchunk=512),
        grid=(t // tm,),
        in_specs=[pl.BlockSpec((tm, d), lambda i: (i, 0)),
                  _const_spec((1, d)), _const_spec((d, n + LANE)),
                  _const_spec((LANE, dk)), _const_spec((1, dk))],
        out_specs=[pl.BlockSpec((tm, n), lambda i: (i, 0)),
                   pl.BlockSpec((tm, dk), lambda i: (i, 0))],
        out_shape=[jax.ShapeDtypeStruct((t, n), BF16), jax.ShapeDtypeStruct((t, dk), F32)],
        compiler_params=_cparams(1), name="gla_proj",
    )(x, gain.reshape(1, d), w_all, w_gu, b_gate.reshape(1, dk))


def _gla_tables():
    c, s = GLA_CHUNK, GLA_SUB
    i = np.arange(c)[:, None]
    j = np.arange(c)[None, :]
    lo = (i // s) * s
    hi = lo + s
    a1 = (j >= lo) & (j <= i)
    a2 = (j > i) & (j < hi)
    a3 = j < lo
    a4 = j >= hi
    amat = np.concatenate([a1, a2, a3, a4], axis=0).astype(np.float32)
    row = np.arange(s * LANE)[:, None] // LANE
    erep = (row == (np.arange(c)[None, :] % s)).astype(np.float32)
    return jnp.asarray(amat, dtype=BF16), jnp.asarray(erep, dtype=BF16)


def _gla_kernel(q_ref, k_ref, v_ref, g_ref, la_ref, gn_ref, amat_ref, erep_ref,
                og_ref, state_s, kf_s, dq_s, tm_s, *, qscale):
    c, s = GLA_CHUNK, GLA_SUB
    nsub = c // s
    nh = state_s.shape[0]
    dk = q_ref.shape[1] // nh
    dv = v_ref.shape[1] // nh

    @pl.when(pl.program_id(1) == 0)
    def _():
        state_s[...] = jnp.zeros_like(state_s)

    il = lax.broadcasted_iota(I32, (s, dk), 0)
    ii = lax.broadcasted_iota(I32, (c, c), 0)
    jj = lax.broadcasted_iota(I32, (c, c), 1)
    ri, rj = ii // s, jj // s
    on_diag = jnp.logical_and(ri == rj, ii >= jj)
    rowblk = lax.broadcasted_iota(I32, (c, dk), 0) // s
    ones_ck = jnp.ones((c, dk), BF16)

    def head_chunk(h, r0, exact_diag):
        kcols = slice(h * dk, (h + 1) * dk)
        vcols = slice(h * dv, (h + 1) * dv)
        la = la_ref[pl.ds(r0, c), kcols]
        qf = q_ref[pl.ds(r0, c), kcols].astype(F32) * qscale
        kf = k_ref[pl.ds(r0, c), kcols].astype(F32)
        vb = v_ref[pl.ds(r0, c), vcols]
        cum = _dot_split(amat_ref[...], la)
        dq, dkk, cprev, ctail = cum[0:c], cum[c:2 * c], cum[2 * c:3 * c], cum[3 * c:4 * c]
        cnext = cprev + dq + dkk
        qt = qf * jnp.exp(dq)
        kt = kf * jnp.exp(dkk)
        if exact_diag:
            kf_s[h] = kf
            dq_s[h] = dq
            for r in range(nsub):
                qb = qf[r * s:(r + 1) * s]
                dqb = dq[r * s:(r + 1) * s]
                for jl in range(s):
                    krow = kf_s[h, pl.ds(r * s + jl, 1), :]
                    drow = dq_s[h, pl.ds(r * s + jl, 1), :]
                    dec = jnp.where(il >= jl, jnp.exp(jnp.minimum(dqb - drow, 0.0)), 0.0)
                    tm_s[h, r * s:(r + 1) * s, jl * dk:(jl + 1) * dk] = (qb * krow * dec).astype(BF16)
            sdiag = jnp.dot(tm_s[h], erep_ref[...], preferred_element_type=F32)
        else:
            kd = (kf * jnp.exp(-dq)).astype(BF16)
            sdiag = lax.dot_general(qt.astype(BF16), kd, (((1,), (1,)), ((), ())),
                                    preferred_element_type=F32)
        soff = jnp.zeros((c, c), F32)
        for rp in range(nsub - 1):
            cn = cnext[rp * s:rp * s + 1, :]
            p = (qt * jnp.exp(jnp.minimum(cprev - cn, 0.0))).astype(BF16)
            kp = jnp.where(rowblk == rp, kt, 0.0).astype(BF16)
            soff = soff + lax.dot_general(p, kp, (((1,), (1,)), ((), ())),
                                          preferred_element_type=F32)
        sc = jnp.where(on_diag, sdiag, jnp.where(ri > rj, soff, 0.0))
        st = state_s[h]
        o = jnp.dot(sc.astype(BF16), vb, preferred_element_type=F32) + jnp.dot(
            (qt * jnp.exp(cprev)).astype(BF16), st.astype(BF16), preferred_element_type=F32)
        ks_t = (kt * jnp.exp(ctail)).T.astype(BF16)
        upd = jnp.dot(ks_t, vb, preferred_element_type=F32)
        dcol = jnp.exp(_dot_split(la.T, ones_ck, split_lhs=True))
        state_s[h] = jnp.concatenate([dcol] * (dv // dk), axis=1) * st + upd
        gf = g_ref[pl.ds(r0, c), vcols].astype(F32)
        og_ref[pl.ds(r0, c), vcols] = (_rms(o, gn_ref[...]) * (gf * _sigmoid(gf))).astype(og_ref.dtype)

    def sweep(exact_diag):
        def chunk(ci, carry):
            r0 = pl.multiple_of(ci * c, c)
            for h in range(nh):
                head_chunk(h, r0, exact_diag)
            return carry

        lax.fori_loop(0, q_ref.shape[0] // c, chunk, 0, unroll=1 if exact_diag else 2)

    safe = jnp.max(-la_ref[...]) * s < GLA_SAFE_EXP
    pl.when(safe)(lambda: sweep(False))
    pl.when(jnp.logical_not(safe))(lambda: sweep(True))


def gla_core(qkvg, la, gnorm, batch, seq, rows=GLA_ROWS):
    t = qkvg.shape[0]
    dkt = la.shape[1]
    dk = dkt // GLA_HEADS
    dvt = (qkvg.shape[1] - 2 * dkt) // 2
    dv = dvt // GLA_HEADS
    nblk = seq // rows
    amat, erep = _gla_tables()
    row = lambda b, c: b * nblk + c
    return pl.pallas_call(
        functools.partial(_gla_kernel, qscale=float(dk) ** -0.5),
        grid=(batch, nblk),
        in_specs=[pl.BlockSpec((rows, dkt), lambda b, c: (row(b, c), 0)),
                  pl.BlockSpec((rows, dkt), lambda b, c: (row(b, c), 1)),
                  pl.BlockSpec((rows, dvt), lambda b, c: (row(b, c), 2 * dkt // dvt)),
                  pl.BlockSpec((rows, dvt), lambda b, c: (row(b, c), 2 * dkt // dvt + 1)),
                  pl.BlockSpec((rows, dkt), lambda b, c: (row(b, c), 0)),
                  _const_spec((1, dv)), _const_spec(amat.shape), _const_spec(erep.shape)],
        out_specs=pl.BlockSpec((rows, dvt), lambda b, c: (row(b, c), 0)),
        out_shape=jax.ShapeDtypeStruct((t, dvt), BF16),
        scratch_shapes=[pltpu.VMEM((GLA_HEADS, dk, dv), F32),
                        pltpu.VMEM((GLA_HEADS, GLA_CHUNK, dk), F32),
                        pltpu.VMEM((GLA_HEADS, GLA_CHUNK, dk), F32),
                        pltpu.VMEM((GLA_HEADS, GLA_CHUNK, GLA_SUB * dk), BF16)],
        compiler_params=_cparams(2), name="gla_core",
    )(qkvg, qkvg, qkvg, qkvg, la, gnorm.reshape(1, dv), amat, erep)


def _sba_kernel(q_ref, k_ref, v_ref, u_ref, o_ref, qs_s, acc_s, carry_s, *, scale, dh):
    bq = SBA_BQ
    nq = q_ref.shape[0] // bq
    step = pl.program_id(2)
    lane = lax.broadcasted_iota(I32, (bq, LANE), 1)
    row = lax.broadcasted_iota(I32, (2 * bq, bq), 0)
    strict = lax.broadcasted_iota(I32, (2 * bq, bq), 1) < jnp.where(row >= bq, row - bq, row)

    def block(jb, diagonal):
        k0 = pl.multiple_of(jb * bq, bq)
        kb = k_ref[pl.ds(k0, bq), :]
        vb = v_ref[pl.ds(k0, bq), :]
        z = lax.dot_general(qs_s[...], kb, (((1,), (1,)), ((), ())), preferred_element_type=F32)
        sp = jnp.maximum(z, 0.0) + jnp.log2(1.0 + jnp.exp2(jnp.minimum(z, -z)))
        if diagonal:
            sp = jnp.where(strict, sp, 0.0)
        ct = jnp.dot(sp.astype(BF16), u_ref[...], preferred_element_type=F32)
        carry = carry_s[...]
        w = jnp.exp2((z - sp) - ct[:, :bq] - jnp.concatenate([carry] * (bq // LANE), axis=1))
        if diagonal:
            w = jnp.where(strict, w, 0.0)
        acc_s[...] += jnp.dot(w.astype(BF16), vb, preferred_element_type=F32)
        carry = carry + ct[:, bq:]
        carry_s[...] = carry
        return jnp.min(carry)

    def qblock(qi, c):
        r0 = pl.multiple_of(qi * bq, bq)
        q2 = q_ref[pl.ds(r0, bq), :].astype(F32) * scale
        qs_s[0:bq] = jnp.where(lane < dh, q2, 0.0).astype(BF16)
        qs_s[bq:2 * bq] = jnp.where(lane >= dh, q2, 0.0).astype(BF16)
        acc_s[...] = jnp.zeros_like(acc_s)
        carry_s[...] = jnp.zeros_like(carry_s)
        jd = step * nq + qi
        low = block(jd, True)
        lax.while_loop(lambda st: jnp.logical_and(st[0] >= 0, st[1] < SBA_DEAD2),
                       lambda st: (st[0] - 1, block(st[0], False)), (jd - 1, low))
        a = acc_s[...]
        o_ref[pl.ds(r0, bq), :] = jnp.where(lane < dh, a[:bq], a[bq:]).astype(o_ref.dtype)
        return c

    lax.fori_loop(0, nq, qblock, 0)


def sba_core(qkv, batch, seq):
    t = qkv.shape[0]
    d = qkv.shape[1] // 3
    dh = d // SBA_HEADS
    npair = d // LANE
    bq = SBA_BQ
    rows = min(SBA_ROWS, seq)
    nstep = seq // rows
    u = np.concatenate([np.tril(np.ones((bq, bq), np.float32), -1), np.ones((bq, LANE), np.float32)], 1)
    return pl.pallas_call(
        functools.partial(_sba_kernel, scale=float(dh) ** -0.5 * LOG2E, dh=dh),
        grid=(batch, npair, nstep),
        in_specs=[pl.BlockSpec((rows, LANE), lambda b, p, i: (b * nstep + i, p)),
                  pl.BlockSpec((seq, LANE), lambda b, p, i: (b, npair + p)),
                  pl.BlockSpec((seq, LANE), lambda b, p, i: (b, 2 * npair + p)),
                  _const_spec((bq, bq + LANE))],
        out_specs=pl.BlockSpec((rows, LANE), lambda b, p, i: (b * nstep + i, p)),
        out_shape=jax.ShapeDtypeStruct((t, d), BF16),
        scratch_shapes=[pltpu.VMEM((2 * bq, LANE), BF16), pltpu.VMEM((2 * bq, LANE), F32),
                        pltpu.VMEM((2 * bq, LANE), F32)],
        compiler_params=_cparams(3), name="sba_core",
    )(qkv, qkv, qkv, jnp.asarray(u, dtype=BF16))


def _pool_kernel(x_ref, halo_ref, g_ref, win_ref, wgrp_ref, sc_ref, o_ref, u_s, *, tiles_per_seq):
    tm, d = x_ref.shape
    hal = halo_ref.shape[0]
    gd = wgrp_ref.shape[1]
    first = (pl.program_id(0) % tiles_per_seq) == 0
    x = x_ref[...]
    xx = jnp.concatenate([halo_ref[...], x], axis=0)
    h = _rms(xx, g_ref[...]).astype(BF16)
    u = jnp.dot(h, win_ref[...], preferred_element_type=F32)
    keep = jnp.where(first, 0.0, 1.0)
    rowi = lax.broadcasted_iota(I32, (hal + tm, 1), 0)
    u_s[...] = u * jnp.where(rowi < hal, keep, 1.0)
    pos = (pl.program_id(0) % tiles_per_seq) * tm + lax.broadcasted_iota(I32, (tm, 1), 0)
    mixed = []
    for gi, w in enumerate(POOL_WINDOWS):
        cs = slice(gi * gd, (gi + 1) * gd)
        ssum = u_s[hal:hal + tm, cs]
        for dlt in range(1, w):
            ssum = ssum + u_s[hal - dlt:hal - dlt + tm, cs]
        cnt = jnp.minimum(pos + 1, w).astype(F32)
        pooled = ssum / cnt - u_s[hal:hal + tm, cs]
        mixed.append(jnp.dot(pooled.astype(BF16), wgrp_ref[gi], preferred_element_type=F32))
    o_ref[...] = x + jnp.concatenate(mixed, axis=1) * sc_ref[...]


def pool_mixer(x, gain, w_in, w_group, scale, seq, tm=ROW_TILE):
    t, d = x.shape
    hal = max(POOL_WINDOWS)
    g, gd, _ = w_group.shape
    return pl.pallas_call(
        functools.partial(_pool_kernel, tiles_per_seq=seq // tm),
        grid=(t // tm,),
        in_specs=[pl.BlockSpec((tm, d), lambda i: (i, 0)),
                  pl.BlockSpec((hal, d), lambda i: (jnp.maximum(i * (tm // hal) - 1, 0), 0)),
                  _const_spec((1, d)), _const_spec((d, d)), _const_spec((g, gd, gd)),
                  _const_spec((1, d))],
        out_specs=pl.BlockSpec((tm, d), lambda i: (i, 0)),
        out_shape=jax.ShapeDtypeStruct((t, d), F32),
        scratch_shapes=[pltpu.VMEM((hal + tm, d), F32)],
        compiler_params=_cparams(1), name="pool_mixer",
    )(x, x, gain.reshape(1, d), w_in, w_group, scale.reshape(1, d))


def _swiglu_acc(h, wg_ref, wu_ref, wd_ref, acc, widx):
    ff = wg_ref.shape[-1]
    for c0 in range(0, ff, FF_CHUNK):
        cs = slice(c0, c0 + FF_CHUNK)
        gate = jnp.dot(h, wg_ref[widx + (slice(None), cs)], preferred_element_type=F32)
        up = jnp.dot(h, wu_ref[widx + (slice(None), cs)], preferred_element_type=F32)
        a = (gate * _sigmoid(gate) * up).astype(BF16)
        acc = acc + jnp.dot(a, wd_ref[widx + (cs, slice(None))], preferred_element_type=F32)
    return acc


def _ffn_kernel(x_ref, a_ref, wo_ref, g_ref, wg_ref, wu_ref, wd_ref, o_ref):
    x = x_ref[...] + jnp.dot(a_ref[...], wo_ref[...], preferred_element_type=F32)
    h = _rms(x, g_ref[...]).astype(BF16)
    o_ref[...] = _swiglu_acc(h, wg_ref, wu_ref, wd_ref, x, ())


def _ffn_plain_kernel(x_ref, g_ref, wg_ref, wu_ref, wd_ref, o_ref):
    x = x_ref[...]
    h = _rms(x, g_ref[...]).astype(BF16)
    o_ref[...] = _swiglu_acc(h, wg_ref, wu_ref, wd_ref, x, ())


def ffn_dense(x, proj, gain, wg, wu, wd, tm=ROW_TILE):
    t, d = x.shape
    ff = wg.shape[1]
    row_spec = pl.BlockSpec((tm, d), lambda i: (i, 0))
    w_specs = [_const_spec((1, d)), _const_spec((d, ff)), _const_spec((d, ff)), _const_spec((ff, d))]
    if proj is None:
        body, in_specs, args = _ffn_plain_kernel, [row_spec], (x,)
    else:
        a, w_out = proj
        ka = a.shape[1]
        body, args = _ffn_kernel, (x, a, w_out)
        in_specs = [row_spec, pl.BlockSpec((tm, ka), lambda i: (i, 0)), _const_spec((ka, d))]
    return pl.pallas_call(
        body,
        grid=(t // tm,),
        in_specs=in_specs + w_specs,
        out_specs=row_spec,
        out_shape=jax.ShapeDtypeStruct((t, d), F32),
        compiler_params=_cparams(1), name="ffn_dense",
    )(*args, gain.reshape(1, d), wg, wu, wd)


def _moe_ffn_kernel(texp_ref, nused_ref, x_ref, g_ref, wg_ref, wu_ref, wd_ref, o_ref):
    live = pl.program_id(0) < nused_ref[0]

    @pl.when(live)
    def _():
        h = _rms(x_ref[...], g_ref[...]).astype(BF16)
        o_ref[...] = _swiglu_acc(h, wg_ref, wu_ref, wd_ref, jnp.zeros(o_ref.shape, F32), (0, 0))

    @pl.when(jnp.logical_not(live))
    def _():
        o_ref[...] = jnp.zeros_like(o_ref)


def moe_ffn(xs, gain, wg, wu, wd, layer, tile_expert, n_used, tm):
    r, d = xs.shape
    ff = wg.shape[3]
    live = lambda i, te, nu: jnp.minimum(i, nu[0] - 1)
    grid_spec = pltpu.PrefetchScalarGridSpec(
        num_scalar_prefetch=2,
        grid=(r // tm,),
        in_specs=[pl.BlockSpec((tm, d), lambda i, te, nu: (live(i, te, nu), 0)),
                  pl.BlockSpec((1, d), lambda i, te, nu: (0, 0)),
                  pl.BlockSpec((1, 1, d, ff), lambda i, te, nu: (layer, te[i], 0, 0)),
                  pl.BlockSpec((1, 1, d, ff), lambda i, te, nu: (layer, te[i], 0, 0)),
                  pl.BlockSpec((1, 1, ff, d), lambda i, te, nu: (layer, te[i], 0, 0))],
        out_specs=pl.BlockSpec((tm, d), lambda i, te, nu: (i, 0)),
    )
    return pl.pallas_call(
        _moe_ffn_kernel, grid_spec=grid_spec,
        out_shape=jax.ShapeDtypeStruct((r, d), F32),
        compiler_params=_cparams(1), name="moe_ffn",
    )(tile_expert, n_used, xs, gain.reshape(1, d), wg, wu, wd)


def _router_kernel(x_ref, a_ref, wo_ref, g_ref, wr_ref, xo_ref, idx_ref, prob_ref, *, ne):
    x = x_ref[...] + jnp.dot(a_ref[...], wo_ref[...], preferred_element_type=F32)
    xo_ref[...] = x
    h = _rms(x, g_ref[...])
    h_hi = h.astype(BF16)
    h_lo = (h - h_hi.astype(F32)).astype(BF16)
    wr = wr_ref[...]
    w_hi = wr.astype(BF16)
    w_lo = (wr - w_hi.astype(F32)).astype(BF16)
    logits = (jnp.dot(h_hi, w_hi, preferred_element_type=F32)
              + jnp.dot(h_lo, w_hi, preferred_element_type=F32)
              + jnp.dot(h_hi, w_lo, preferred_element_type=F32))
    col = lax.broadcasted_iota(I32, logits.shape, 1).astype(F32)
    logits = jnp.where(col < ne, logits, -jnp.inf)
    m1 = jnp.max(logits, axis=1, keepdims=True)
    i1 = jnp.min(jnp.where(logits == m1, col, float(ne)), axis=1, keepdims=True)
    rest = jnp.where(col == i1, -jnp.inf, logits)
    m2 = jnp.max(rest, axis=1, keepdims=True)
    i2 = jnp.min(jnp.where(rest == m2, col, float(ne)), axis=1, keepdims=True)
    e2 = jnp.exp(m2 - m1)
    p1 = 1.0 / (1.0 + e2)
    idx_ref[...] = jnp.where(col == 0.0, i1, i2).astype(I32)
    prob_ref[...] = jnp.where(col == 0.0, p1, e2 * p1)


def router_top2(x, a, w_out, gain, wr, tm=ROW_TILE):
    t, d = x.shape
    ka = a.shape[1]
    ne = wr.shape[1]
    wr_pad = jnp.zeros((d, LANE), F32).at[:, :ne].set(wr)
    return pl.pallas_call(
        functools.partial(_router_kernel, ne=ne),
        grid=(t // tm,),
        in_specs=[pl.BlockSpec((tm, d), lambda i: (i, 0)), pl.BlockSpec((tm, ka), lambda i: (i, 0)),
                  _const_spec((ka, d)), _const_spec((1, d)), _const_spec((d, LANE))],
        out_specs=[pl.BlockSpec((tm, d), lambda i: (i, 0)),
                   pl.BlockSpec((tm, LANE), lambda i: (i, 0)), pl.BlockSpec((tm, LANE), lambda i: (i, 0))],
        out_shape=[jax.ShapeDtypeStruct((t, d), F32),
                   jax.ShapeDtypeStruct((t, LANE), I32), jax.ShapeDtypeStruct((t, LANE), F32)],
        compiler_params=_cparams(1), name="router_top2",
    )(x, a, w_out, gain.reshape(1, d), wr_pad)


def _dispatch_kernel(pos_ref, x_ref, xs_in, xs_hbm, sem):
    del xs_in
    tm = x_ref.shape[0]

    def issue(k, c):
        pltpu.make_async_copy(x_ref.at[pl.ds(k, 1)], xs_hbm.at[pl.ds(pos_ref[0, 0, k], 1)], sem).start()
        pltpu.make_async_copy(x_ref.at[pl.ds(k, 1)], xs_hbm.at[pl.ds(pos_ref[0, 1, k], 1)], sem).start(priority=1)
        return c

    lax.fori_loop(0, tm, issue, 0, unroll=8)
    pltpu.make_async_copy(x_ref, xs_hbm.at[pl.ds(0, tm)], sem).wait()
    pltpu.make_async_copy(x_ref, xs_hbm.at[pl.ds(0, tm)], sem).wait()


def dispatch_rows(x, pos3, seed):
    t, d = x.shape
    n_rows = seed.shape[0]
    tm = pos3.shape[2]
    return pl.pallas_call(
        _dispatch_kernel,
        grid=(t // tm,),
        in_specs=[pl.BlockSpec((1, 2, tm), lambda i: (i, 0, 0), memory_space=pltpu.SMEM),
                  pl.BlockSpec((tm, d), lambda i: (i, 0)),
                  pl.BlockSpec(memory_space=pl.ANY)],
        out_specs=pl.BlockSpec(memory_space=pl.ANY),
        out_shape=jax.ShapeDtypeStruct((n_rows, d), x.dtype),
        input_output_aliases={2: 0},
        scratch_shapes=[pltpu.SemaphoreType.DMA(())],
        compiler_params=_cparams(1), name="dispatch_rows",
    )(pos3, x, seed)


def _combine_kernel(pos_ref, x_ref, p_ref, g_ref, y_hbm, o_ref, buf, sem, *, final_norm):
    tm = x_ref.shape[0]

    def issue(k, c):
        pltpu.make_async_copy(y_hbm.at[pl.ds(pos_ref[0, 0, k], 1)], buf.at[0, pl.ds(k, 1)], sem).start()
        pltpu.make_async_copy(y_hbm.at[pl.ds(pos_ref[0, 1, k], 1)], buf.at[1, pl.ds(k, 1)], sem).start(priority=1)
        return c

    lax.fori_loop(0, tm, issue, 0, unroll=8)
    pltpu.make_async_copy(y_hbm.at[pl.ds(0, tm)], buf.at[0], sem).wait()
    pltpu.make_async_copy(y_hbm.at[pl.ds(0, tm)], buf.at[1], sem).wait()
    p = p_ref[...]
    out = x_ref[...] + p[:, 0:1] * buf[0] + p[:, 1:2] * buf[1]
    if final_norm:
        out = _rms(out, g_ref[...])
    o_ref[...] = out


def moe_combine(x, ys, pos3, prob, final_gain):
    t, d = x.shape
    final_norm = final_gain is not None
    gain = final_gain if final_norm else jnp.ones((d,), F32)
    tm = pos3.shape[2]
    return pl.pallas_call(
        functools.partial(_combine_kernel, final_norm=final_norm),
        grid=(t // tm,),
        in_specs=[pl.BlockSpec((1, 2, tm), lambda i: (i, 0, 0), memory_space=pltpu.SMEM),
                  pl.BlockSpec((tm, d), lambda i: (i, 0)),
                  pl.BlockSpec((tm, LANE), lambda i: (i, 0)),
                  _const_spec((1, d)),
                  pl.BlockSpec(memory_space=pl.ANY)],
        out_specs=pl.BlockSpec((tm, d), lambda i: (i, 0)),
        out_shape=jax.ShapeDtypeStruct((t, d), F32),
        scratch_shapes=[pltpu.VMEM((2, tm, d), F32), pltpu.SemaphoreType.DMA(())],
        compiler_params=_cparams(1), name="moe_combine",
    )(pos3, x, prob, gain.reshape(1, d), ys)


def _final_norm_kernel(x_ref, g_ref, o_ref):
    o_ref[...] = _rms(x_ref[...], g_ref[...])


def final_norm(x, gain, tm=ROW_TILE):
    t, d = x.shape
    return pl.pallas_call(
        _final_norm_kernel, grid=(t // tm,),
        in_specs=[pl.BlockSpec((tm, d), lambda i: (i, 0)), _const_spec((1, d))],
        out_specs=pl.BlockSpec((tm, d), lambda i: (i, 0)),
        out_shape=jax.ShapeDtypeStruct((t, d), F32),
        compiler_params=_cparams(1), name="final_norm",
    )(x, gain.reshape(1, d))


def _route_plan(idx0, idx1, tm, tc):
    t = idx0.shape[0]
    e_flat = jnp.concatenate([idx0, idx1])
    onehot = (jnp.arange(N_EXPERTS, dtype=I32)[:, None] == e_flat[None, :]).astype(I32)
    csum = jnp.cumsum(onehot, axis=1)
    counts = csum[:, -1]
    padded = ((counts + tm - 1) // tm) * tm
    ends = jnp.cumsum(padded)
    starts = ends - padded
    pos = jnp.sum(onehot * (csum - 1 + starts[:, None]), axis=0)
    n_rows = 2 * t + N_EXPERTS * tm
    n_used = (ends[-1] // tm).astype(I32)
    tile_start = jnp.arange(n_rows // tm, dtype=I32) * tm
    tile_start = jnp.minimum(tile_start, ends[-1] - 1)
    tile_expert = jnp.sum((ends[None, :] <= tile_start[:, None]).astype(I32), axis=1)
    tile_expert = jnp.minimum(tile_expert, N_EXPERTS - 1)
    pos3 = jnp.stack([pos[:t].reshape(t // tc, tc), pos[t:].reshape(t // tc, tc)], axis=1)
    return pos3, tile_expert, n_used.reshape(1), n_rows


def moe_layer(x, proj, gain, router, wg, wu, wd, layer, final_gain, seed=None, tm=MOE_TILE, tc=DISPATCH_ROWS):
    x, idx, prob = router_top2(x, proj[0], proj[1], gain, router)
    pos3, tile_expert, n_used, n_rows = _route_plan(idx[:, 0], idx[:, 1], tm, tc)
    if seed is None:
        seed = jnp.zeros((n_rows, x.shape[1]), x.dtype)
    xs = dispatch_rows(x, pos3, seed)
    ys = moe_ffn(xs, gain, wg, wu, wd, layer, tile_expert, n_used, tm)
    return moe_combine(x, ys, pos3, prob, final_gain), xs


def gla_layer(x, gain, w_in, w_gate_up, b_gate, gnorm, w_out, batch, seq):
    d = x.shape[1]
    n_main = w_in.shape[1] - GLA_GATE_RANK
    w_all = jnp.zeros((d, n_main + LANE), BF16).at[:, :w_in.shape[1]].set(w_in.astype(BF16))
    w_gu = jnp.zeros((LANE, w_gate_up.shape[1]), F32).at[:GLA_GATE_RANK].set(w_gate_up)
    qkvg, la = gla_proj(x, gain, w_all, w_gu, b_gate)
    og = gla_core(qkvg, la, gnorm, batch, seq, rows=min(GLA_ROWS, seq))
    return og, w_out.astype(BF16)


def sba_layer(x, gain, w_in, w_out, batch, seq):
    qkv = norm_proj(x, gain, w_in.astype(BF16))
    return sba_core(qkv, batch, seq), w_out.astype(BF16)


def kernel(x, norm_mix, norm_ffn, norm_final, gla_w_in, gla_w_gate_up, gla_b_gate, gla_norm, gla_w_out, sba_w_in, sba_w_out, pool_w_in, pool_w_group, pool_scale, ffn_w_gate, ffn_w_up, ffn_w_down, moe_router, moe_w_gate, moe_w_up, moe_w_down):
    batch, seq, d = x.shape
    depth = norm_mix.shape[0]
    x = x.reshape(batch * seq, d)
    normed = False
    sorted_rows = None
    moe_wg, moe_wu, moe_wd = moe_w_gate.astype(BF16), moe_w_up.astype(BF16), moe_w_down.astype(BF16)
    for i in range(depth):
        kind, j = i % 3, i // 3
        if kind == 0:
            proj = gla_layer(x, norm_mix[i], gla_w_in[j], gla_w_gate_up[j], gla_b_gate[j],
                             gla_norm[j], gla_w_out[j], batch, seq)
        elif kind == 1:
            proj = sba_layer(x, norm_mix[i], sba_w_in[j], sba_w_out[j], batch, seq)
        else:
            proj = None
            x = pool_mixer(x, norm_mix[i], pool_w_in[j].astype(BF16), pool_w_group[j].astype(BF16),
                           pool_scale[j], seq)
        f = i // 2
        if i % 2 == 0:
            x = ffn_dense(x, proj, norm_ffn[i], ffn_w_gate[f].astype(BF16), ffn_w_up[f].astype(BF16),
                          ffn_w_down[f].astype(BF16))
        else:
            if proj is None:
                raise NotImplementedError("pooling mixer followed by the routed block")
            last = i == depth - 1
            x, sorted_rows = moe_layer(x, proj, norm_ffn[i], moe_router[f], moe_wg, moe_wu, moe_wd, f,
                                       norm_final if last else None, seed=sorted_rows)
            normed = last
    if not normed:
        x = final_norm(x, norm_final)
    return x.reshape(batch, seq, d)
```
